```python
import jax, jax.numpy as jnp
from jax import lax
import numpy as np

D_MODEL = 2048
BATCH = 4
SEQ = 2048
DEPTH = 2
DEC_BATCH = 128
DEC_SEQ = 1
PAST_LEN = 16384
PAGE_SIZE = 128

HEAD_DIM = 128
POOL_WIDTH = D_MODEL // 4
POOL_WINDOWS = (2, 4, 8, 16)
POOL_GW = POOL_WIDTH // len(POOL_WINDOWS)
POOL_BUF = max(POOL_WINDOWS) - 1
CONV_WIDTH = (D_MODEL - POOL_WIDTH) // 2
CONF_K = 31
SC_WIDTH = D_MODEL - POOL_WIDTH - CONV_WIDTH
SC_K = 3
FFN_K = 3
D_FF = ((8 * D_MODEL // 3 + 255) // 256) * 256
PLE_DIM = 256
IN_COLS = POOL_WIDTH + 2 * CONV_WIDTH + 3 * SC_WIDTH
EPS = 1e-6

kernel_name = "hybrid_pool_conformer_shortconv_decoder_step"


def rmsnorm(x, g):
    xf = x.astype(jnp.float32)
    y = xf * lax.rsqrt(jnp.mean(xf * xf, axis=-1, keepdims=True) + EPS)
    return (y * g.astype(jnp.float32)).astype(x.dtype)


def layernorm(x, g, b):
    xf = x.astype(jnp.float32)
    mu = jnp.mean(xf, axis=-1, keepdims=True)
    var = jnp.mean(jnp.square(xf - mu), axis=-1, keepdims=True)
    y = (xf - mu) * lax.rsqrt(var + EPS) * g.astype(jnp.float32) + b.astype(jnp.float32)
    return y.astype(x.dtype)


def causal_dwconv(buf, x, w):
    k = w.shape[0]
    xp = jnp.concatenate([buf.astype(x.dtype), x], axis=1)
    y = lax.conv_general_dilated(
        xp, w.astype(x.dtype)[:, None, :], window_strides=(1,), padding='VALID',
        dimension_numbers=('NWC', 'WIO', 'NWC'), feature_group_count=x.shape[-1])
    return y, xp[:, -(k - 1):]


def pool_mixer(u, buf, start, w_pool, scale):
    b, s, c = u.shape
    cat = jnp.concatenate([buf.astype(u.dtype), u], axis=1)
    cs = jnp.cumsum(cat.astype(jnp.float32), axis=1)
    cs = jnp.concatenate([jnp.zeros((b, 1, c), jnp.float32), cs], axis=1)
    pos = start + jnp.arange(s, dtype=jnp.int32)
    means = []
    for g, w in enumerate(POOL_WINDOWS):
        sl = slice(g * POOL_GW, (g + 1) * POOL_GW)
        wsum = cs[:, POOL_BUF + 1:, sl] - cs[:, POOL_BUF + 1 - w:POOL_BUF + 1 - w + s, sl]
        cnt = jnp.minimum(w, pos + 1).astype(jnp.float32)[None, :, None]
        means.append(wsum / cnt)
    d = (jnp.concatenate(means, axis=-1) - u.astype(jnp.float32)).astype(u.dtype)
    d = d.reshape(b, s, len(POOL_WINDOWS), POOL_GW)
    y = jnp.einsum('bsgc,gcd->bsgd', d, w_pool).reshape(b, s, c) * scale
    return y, cat[:, -POOL_BUF:]


def trunk_layer(h, p_i, st_pool, st_conf, st_sc, st_ffn, start,
                norm_mix, w_in, w_pool, pool_scale, conf_dw, conf_dw_b, conf_ln_g, conf_ln_b,
                conf_pw, conf_pw_b, sc_conv, w_out, norm_ffn, w_up, ffn_conv, w_down,
                norm_ple, ple_gate, ple_proj):
    hn = rmsnorm(h, norm_mix)
    z = hn @ w_in
    o = 0
    u_a = z[..., o:o + POOL_WIDTH]; o += POOL_WIDTH
    glu_a = z[..., o:o + CONV_WIDTH]; o += CONV_WIDTH
    glu_b = z[..., o:o + CONV_WIDTH]; o += CONV_WIDTH
    sc_b = z[..., o:o + SC_WIDTH]; o += SC_WIDTH
    sc_c = z[..., o:o + SC_WIDTH]; o += SC_WIDTH
    sc_h = z[..., o:o + SC_WIDTH]

    y_a, new_pool = pool_mixer(u_a, st_pool, start, w_pool, pool_scale)

    g = glu_a * jax.nn.sigmoid(glu_b)
    cb, new_conf = causal_dwconv(st_conf, g, conf_dw)
    cb = layernorm(cb + conf_dw_b, conf_ln_g, conf_ln_b)
    y_b = jax.nn.silu(cb) @ conf_pw + conf_pw_b

    v = sc_c * sc_h
    cv, new_sc = causal_dwconv(st_sc, v, sc_conv)
    y_c = sc_b * cv

    h = h + jnp.concatenate([y_a, y_b, y_c], axis=-1) @ w_out

    hn = rmsnorm(h, norm_ffn)
    up = hn @ w_up
    upc, new_ffn = causal_dwconv(st_ffn, up, ffn_conv)
    h = h + (jax.nn.silu(upc[..., :D_FF]) * upc[..., D_FF:]) @ w_down

    gate = jax.nn.sigmoid(rmsnorm(h, norm_ple) @ ple_gate)
    h = h + (p_i @ ple_proj) * gate
    return h, new_pool, new_conf, new_sc, new_ffn


def setup_inputs(seed: int = 0) -> dict:
    key = jax.random.key(seed)
    ks = jax.random.split(key, 32)
    f = jnp.float32
    nrm = lambda k, shape, s: jax.random.normal(k, shape, f) * s
    return {
        "x_prompt": nrm(ks[0], (BATCH, SEQ, D_MODEL), 1.0),
        "x_sample": nrm(ks[1], (DEC_BATCH, DEC_SEQ, D_MODEL), 1.0),
        "p_prompt": nrm(ks[2], (DEPTH, BATCH, SEQ, PLE_DIM), 1.0),
        "p_sample": nrm(ks[3], (DEPTH, DEC_BATCH, DEC_SEQ, PLE_DIM), 1.0),
        "state_pool": nrm(ks[4], (DEPTH, DEC_BATCH, POOL_BUF, POOL_WIDTH), 1.0),
        "state_conf": nrm(ks[5], (DEPTH, DEC_BATCH, CONF_K - 1, CONV_WIDTH), 0.5),
        "state_sc": nrm(ks[6], (DEPTH, DEC_BATCH, SC_K - 1, SC_WIDTH), 0.5),
        "state_ffn": nrm(ks[7], (DEPTH, DEC_BATCH, FFN_K - 1, 2 * D_FF), 1.0),
        "norm_mix": 1.0 + nrm(ks[8], (DEPTH, D_MODEL), 0.02),
        "w_in": nrm(ks[9], (DEPTH, D_MODEL, IN_COLS), D_MODEL ** -0.5),
        "w_pool": nrm(ks[10], (DEPTH, len(POOL_WINDOWS), POOL_GW, POOL_GW), POOL_GW ** -0.5),
        "pool_scale": 1.0 + nrm(ks[11], (DEPTH, POOL_WIDTH), 0.02),
        "conf_dw": nrm(ks[12], (DEPTH, CONF_K, CONV_WIDTH), CONF_K ** -0.5),
        "conf_dw_b": nrm(ks[13], (DEPTH, CONV_WIDTH), 0.02),
        "conf_ln_g": 1.0 + nrm(ks[14], (DEPTH, CONV_WIDTH), 0.02),
        "conf_ln_b": nrm(ks[15], (DEPTH, CONV_WIDTH), 0.02),
        "conf_pw": nrm(ks[16], (DEPTH, CONV_WIDTH, CONV_WIDTH), CONV_WIDTH ** -0.5),
        "conf_pw_b": nrm(ks[17], (DEPTH, CONV_WIDTH), 0.02),
        "sc_conv": nrm(ks[18], (DEPTH, SC_K, SC_WIDTH), SC_K ** -0.5),
        "w_out": nrm(ks[19], (DEPTH, D_MODEL, D_MODEL), D_MODEL ** -0.5),
        "norm_ffn": 1.0 + nrm(ks[20], (DEPTH, D_MODEL), 0.02),
        "w_up": nrm(ks[21], (DEPTH, D_MODEL, 2 * D_FF), D_MODEL ** -0.5),
        "ffn_conv": nrm(ks[22], (DEPTH, FFN_K, 2 * D_FF), FFN_K ** -0.5),
        "w_down": nrm(ks[23], (DEPTH, D_FF, D_MODEL), D_FF ** -0.5),
        "norm_ple": 1.0 + nrm(ks[24], (DEPTH, D_MODEL), 0.02),
        "ple_gate": nrm(ks[25], (DEPTH, D_MODEL, D_MODEL), D_MODEL ** -0.5),
        "ple_proj": nrm(ks[26], (DEPTH, PLE_DIM, D_MODEL), PLE_DIM ** -0.5),
        "norm_final": 1.0 + nrm(ks[27], (D_MODEL,), 0.02),
    }


def reference(x_prompt, x_sample, p_prompt, p_sample, state_pool, state_conf, state_sc, state_ffn,
              norm_mix, w_in, w_pool, pool_scale, conf_dw, conf_dw_b, conf_ln_g, conf_ln_b,
              conf_pw, conf_pw_b, sc_conv, w_out, norm_ffn, w_up, ffn_conv, w_down,
              norm_ple, ple_gate, ple_proj, norm_final):
    dt = x_prompt.dtype
    b = x_prompt.shape[0]
    z_pool = jnp.zeros((b, POOL_BUF, POOL_WIDTH), dt)
    z_conf = jnp.zeros((b, CONF_K - 1, CONV_WIDTH), dt)
    z_sc = jnp.zeros((b, SC_K - 1, SC_WIDTH), dt)
    z_ffn = jnp.zeros((b, FFN_K - 1, 2 * D_FF), dt)

    hp, hs = x_prompt, x_sample
    pp_l, ps_l, cp_l, cs_l, sp_l, ss_l, fp_l, fs_l = [], [], [], [], [], [], [], []
    for i in range(DEPTH):
        w = (norm_mix[i], w_in[i], w_pool[i], pool_scale[i], conf_dw[i], conf_dw_b[i],
             conf_ln_g[i], conf_ln_b[i], conf_pw[i], conf_pw_b[i], sc_conv[i], w_out[i],
             norm_ffn[i], w_up[i], ffn_conv[i], w_down[i], norm_ple[i], ple_gate[i], ple_proj[i])
        hp, a1, a2, a3, a4 = trunk_layer(hp, p_prompt[i], z_pool, z_conf, z_sc, z_ffn, 0, *w)
        hs, b1, b2, b3, b4 = trunk_layer(hs, p_sample[i], state_pool[i], state_conf[i],
                                         state_sc[i], state_ffn[i], PAST_LEN, *w)
        pp_l.append(a1); cp_l.append(a2); sp_l.append(a3); fp_l.append(a4)
        ps_l.append(b1); cs_l.append(b2); ss_l.append(b3); fs_l.append(b4)

    y_prompt = rmsnorm(hp, norm_final)
    y_sample = rmsnorm(hs, norm_final)
    return (y_prompt, y_sample,
            jnp.stack(pp_l), jnp.stack(ps_l),
            jnp.stack(cp_l), jnp.stack(cs_l),
            jnp.stack(sp_l), jnp.stack(ss_l),
            jnp.stack(fp_l), jnp.stack(fs_l))
```

```python
import functools

import jax
import jax.numpy as jnp
from jax import lax
from jax.experimental import pallas as pl
from jax.experimental.pallas import tpu as pltpu

D_MODEL = 2048
POOL_WIDTH = 512
POOL_WINDOWS = (2, 4, 8, 16)
POOL_GW = 128
POOL_BUF = 15
CONV_WIDTH = 768
CONF_K = 31
SC_WIDTH = 768
SC_K = 3
FFN_K = 3
D_FF = 5632
PLE_DIM = 256
IN_COLS = POOL_WIDTH + 2 * CONV_WIDTH + 3 * SC_WIDTH
EPS = 1e-6
PAST_LEN = 16384

_O_U = 0
_O_GA = _O_U + POOL_WIDTH
_O_GB = _O_GA + CONV_WIDTH
_O_SB = _O_GB + CONV_WIDTH
_O_SC = _O_SB + SC_WIDTH
_O_SH = _O_SC + SC_WIDTH

_VMEM_LIMIT = 56 * 1024 * 1024
_LANE = 128
_SUBLANE = 8

_HALO_POOL = 16
_HALO_CONF = 32
_HALO_SHORT = 8


def _params(*sem):
    return pltpu.CompilerParams(dimension_semantics=sem, vmem_limit_bytes=_VMEM_LIMIT)


def _rms(x, g):
    ms = jnp.mean(x * x, axis=-1, keepdims=True)
    return x * lax.rsqrt(ms + EPS) * g


def _dot(a, b):
    return jnp.dot(a, b, preferred_element_type=jnp.float32)


def _silu(x):
    return x * jax.nn.sigmoid(x)


def _single(block, index_map):
    return pl.BlockSpec(block, index_map, pipeline_mode=pl.Buffered(1))


def _norm_matmul_kernel(x_ref, g_ref, w_ref, o_ref):
    o_ref[...] = _dot(_rms(x_ref[...], g_ref[...]), w_ref[...])


def _norm_matmul(x, g, w, *, tm, tn, name):
    m, k = x.shape
    n = w.shape[1]
    return pl.pallas_call(
        _norm_matmul_kernel,
        out_shape=jax.ShapeDtypeStruct((m, n), jnp.float32),
        grid=(n // tn, m // tm),
        in_specs=[
            pl.BlockSpec((tm, k), lambda j, i: (i, 0)),
            pl.BlockSpec((1, k), lambda j, i: (0, 0)),
            _single((k, tn), lambda j, i: (0, j)),
        ],
        out_specs=pl.BlockSpec((tm, tn), lambda j, i: (i, j)),
        compiler_params=_params("arbitrary", "arbitrary"),
        name=name,
    )(x, g.reshape(1, k), w)


def _matmul_res_kernel(x_ref, w_ref, r_ref, o_ref):
    o_ref[...] = r_ref[...] + _dot(x_ref[...], w_ref[...])


def _matmul_res(x, w, res, *, tm, name):
    m, k = x.shape
    n = w.shape[1]
    return pl.pallas_call(
        _matmul_res_kernel,
        out_shape=jax.ShapeDtypeStruct((m, n), jnp.float32),
        grid=(m // tm,),
        in_specs=[
            pl.BlockSpec((tm, k), lambda i: (i, 0)),
            _single((k, n), lambda i: (0, 0)),
            pl.BlockSpec((tm, n), lambda i: (i, 0)),
        ],
        out_specs=pl.BlockSpec((tm, n), lambda i: (i, 0)),
        compiler_params=_params("arbitrary"),
        name=name,
    )(x, w, res)


def _mixer_prompt_kernel(z_ref, wpool_ref, pscale_ref, dw_ref, dwb_ref, lng_ref, lnb_ref,
                         pw_ref, pwb_ref, scw_ref,
                         y_ref, nconf_ref, nsc_ref,
                         eu_ref, eg_ref, ev_ref, *, tm, n_s):
    s = pl.program_id(1)

    @pl.when(s == 0)
    def _():
        eu_ref[0:_HALO_POOL, :] = jnp.zeros((_HALO_POOL, POOL_WIDTH), jnp.float32)
        eg_ref[0:_HALO_CONF, :] = jnp.zeros((_HALO_CONF, CONV_WIDTH), jnp.float32)
        ev_ref[0:_HALO_SHORT, :] = jnp.zeros((_HALO_SHORT, SC_WIDTH), jnp.float32)

    eu_ref[_HALO_POOL:_HALO_POOL + tm, :] = z_ref[:, _O_U:_O_U + POOL_WIDTH]
    pos = s * tm + lax.broadcasted_iota(jnp.int32, (tm, 1), 0)
    for g, w in enumerate(POOL_WINDOWS):
        c0 = g * POOL_GW
        u = eu_ref[_HALO_POOL:_HALO_POOL + tm, c0:c0 + POOL_GW]
        wsum = u
        for k in range(1, w):
            wsum = wsum + eu_ref[_HALO_POOL - k:_HALO_POOL - k + tm, c0:c0 + POOL_GW]
        cnt = jnp.minimum(w, pos + 1).astype(jnp.float32)
        d = wsum / cnt - u
        y_ref[:, c0:c0 + POOL_GW] = _dot(d, wpool_ref[g]) * pscale_ref[:, c0:c0 + POOL_GW]
    eu_ref[0:_HALO_POOL, :] = eu_ref[tm:tm + _HALO_POOL, :]

    eg_ref[_HALO_CONF:_HALO_CONF + tm, :] = (
        z_ref[:, _O_GA:_O_GA + CONV_WIDTH] * jax.nn.sigmoid(z_ref[:, _O_GB:_O_GB + CONV_WIDTH]))
    base = _HALO_CONF - (CONF_K - 1)
    for c0 in range(0, CONV_WIDTH, _LANE):
        acc = dw_ref[0:1, c0:c0 + _LANE] * eg_ref[base:base + tm, c0:c0 + _LANE]
        for k in range(1, CONF_K):
            acc = acc + dw_ref[k:k + 1, c0:c0 + _LANE] * eg_ref[base + k:base + k + tm, c0:c0 + _LANE]
        ev_ref[_HALO_SHORT:_HALO_SHORT + tm, c0:c0 + _LANE] = acc + dwb_ref[:, c0:c0 + _LANE]
    cb = ev_ref[_HALO_SHORT:_HALO_SHORT + tm, :]
    mu = jnp.mean(cb, axis=-1, keepdims=True)
    xc = cb - mu
    var = jnp.mean(xc * xc, axis=-1, keepdims=True)
    ln = xc * lax.rsqrt(var + EPS) * lng_ref[...] + lnb_ref[...]
    y_ref[:, POOL_WIDTH:POOL_WIDTH + CONV_WIDTH] = _dot(_silu(ln), pw_ref[...]) + pwb_ref[...]

    @pl.when(s == n_s - 1)
    def _():
        nconf_ref[0] = eg_ref[tm + base:tm + _HALO_CONF, :]

    eg_ref[0:_HALO_CONF, :] = eg_ref[tm:tm + _HALO_CONF, :]

    ev_ref[_HALO_SHORT:_HALO_SHORT + tm, :] = (
        z_ref[:, _O_SC:_O_SC + SC_WIDTH] * z_ref[:, _O_SH:_O_SH + SC_WIDTH])
    cv = scw_ref[0:1, :] * ev_ref[_HALO_SHORT - 2:_HALO_SHORT - 2 + tm, :]
    cv = cv + scw_ref[1:2, :] * ev_ref[_HALO_SHORT - 1:_HALO_SHORT - 1 + tm, :]
    cv = cv + scw_ref[2:3, :] * ev_ref[_HALO_SHORT:_HALO_SHORT + tm, :]
    y_ref[:, POOL_WIDTH + CONV_WIDTH:D_MODEL] = z_ref[:, _O_SB:_O_SB + SC_WIDTH] * cv

    @pl.when(s == n_s - 1)
    def _():
        nsc_ref[0] = ev_ref[tm + _HALO_SHORT - (SC_K - 1):tm + _HALO_SHORT, :]

    ev_ref[0:_HALO_SHORT, :] = ev_ref[tm:tm + _HALO_SHORT, :]


def _mixer_prompt(z, nb, w_pool, pool_scale, conf_dw, conf_dw_b, conf_ln_g, conf_ln_b,
                  conf_pw, conf_pw_b, sc_conv, *, tm):
    m = z.shape[0]
    n_s = m // nb // tm
    row = lambda b, s: (b * n_s + s, 0)
    const2 = lambda b, s: (0, 0)
    const3 = lambda b, s: (0, 0, 0)
    vec = lambda a: a.reshape(1, -1)
    return pl.pallas_call(
        functools.partial(_mixer_prompt_kernel, tm=tm, n_s=n_s),
        out_shape=(
            jax.ShapeDtypeStruct((m, D_MODEL), jnp.float32),
            jax.ShapeDtypeStruct((nb, CONF_K - 1, CONV_WIDTH), jnp.float32),
            jax.ShapeDtypeStruct((nb, SC_K - 1, SC_WIDTH), jnp.float32),
        ),
        grid=(nb, n_s),
        in_specs=[
            pl.BlockSpec((tm, IN_COLS), row),
            pl.BlockSpec(w_pool.shape, const3),
            pl.BlockSpec((1, POOL_WIDTH), const2),
            pl.BlockSpec((CONF_K, CONV_WIDTH), const2),
            pl.BlockSpec((1, CONV_WIDTH), const2),
            pl.BlockSpec((1, CONV_WIDTH), const2),
            pl.BlockSpec((1, CONV_WIDTH), const2),
            pl.BlockSpec((CONV_WIDTH, CONV_WIDTH), const2),
            pl.BlockSpec((1, CONV_WIDTH), const2),
            pl.BlockSpec((SC_K, SC_WIDTH), const2),
        ],
        out_specs=(
            pl.BlockSpec((tm, D_MODEL), row),
            pl.BlockSpec((1, CONF_K - 1, CONV_WIDTH), lambda b, s: (b, 0, 0)),
            pl.BlockSpec((1, SC_K - 1, SC_WIDTH), lambda b, s: (b, 0, 0)),
        ),
        scratch_shapes=[
            pltpu.VMEM((_HALO_POOL + tm, POOL_WIDTH), jnp.float32),
            pltpu.VMEM((_HALO_CONF + tm, CONV_WIDTH), jnp.float32),
            pltpu.VMEM((_HALO_SHORT + tm, SC_WIDTH), jnp.float32),
        ],
        compiler_params=_params("arbitrary", "arbitrary"),
        name="mixer_prompt",
    )(z, w_pool, vec(pool_scale), conf_dw, vec(conf_dw_b), vec(conf_ln_g), vec(conf_ln_b),
      conf_pw, vec(conf_pw_b), sc_conv)


def _mixer_sample_kernel(z_ref, spool_ref, sconf_ref, ssc_ref,
                         wpool_ref, pscale_ref, dw_ref, dwb_ref, lng_ref, lnb_ref,
                         pw_ref, pwb_ref, scw_ref,
                         y_ref, g_ref, v_ref):
    for g, w in enumerate(POOL_WINDOWS):
        c0 = g * POOL_GW
        u = z_ref[:, _O_U + c0:_O_U + c0 + POOL_GW]
        wsum = u
        for k in range(1, w):
            wsum = wsum + spool_ref[POOL_BUF - k, :, c0:c0 + POOL_GW]
        d = wsum / jnp.float32(min(w, PAST_LEN + 1)) - u
        y_ref[:, c0:c0 + POOL_GW] = _dot(d, wpool_ref[g]) * pscale_ref[:, c0:c0 + POOL_GW]

    gl = z_ref[:, _O_GA:_O_GA + CONV_WIDTH] * jax.nn.sigmoid(z_ref[:, _O_GB:_O_GB + CONV_WIDTH])
    g_ref[...] = gl
    acc = dw_ref[CONF_K - 1:CONF_K, :] * gl
    for k in range(CONF_K - 1):
        acc = acc + dw_ref[k:k + 1, :] * sconf_ref[k]
    cb = acc + dwb_ref[...]
    mu = jnp.mean(cb, axis=-1, keepdims=True)
    xc = cb - mu
    var = jnp.mean(xc * xc, axis=-1, keepdims=True)
    ln = xc * lax.rsqrt(var + EPS) * lng_ref[...] + lnb_ref[...]
    y_ref[:, POOL_WIDTH:POOL_WIDTH + CONV_WIDTH] = _dot(_silu(ln), pw_ref[...]) + pwb_ref[...]

    v = z_ref[:, _O_SC:_O_SC + SC_WIDTH] * z_ref[:, _O_SH:_O_SH + SC_WIDTH]
    v_ref[...] = v
    cv = scw_ref[0:1, :] * ssc_ref[0] + scw_ref[1:2, :] * ssc_ref[1] + scw_ref[2:3, :] * v
    y_ref[:, POOL_WIDTH + CONV_WIDTH:D_MODEL] = z_ref[:, _O_SB:_O_SB + SC_WIDTH] * cv


def _mixer_sample(z, spool_t, sconf_t, ssc_t, w_pool, pool_scale, conf_dw, conf_dw_b,
                  conf_ln_g, conf_ln_b, conf_pw, conf_pw_b, sc_conv):
    m = z.shape[0]
    vec = lambda a: a.reshape(1, -1)
    return pl.pallas_call(
        _mixer_sample_kernel,
        out_shape=(
            jax.ShapeDtypeStruct((m, D_MODEL), jnp.float32),
            jax.ShapeDtypeStruct((m, CONV_WIDTH), jnp.float32),
            jax.ShapeDtypeStruct((m, SC_WIDTH), jnp.float32),
        ),
        compiler_params=pltpu.CompilerParams(vmem_limit_bytes=_VMEM_LIMIT),
        name="mixer_sample",
    )(z, spool_t, sconf_t, ssc_t, w_pool, vec(pool_scale), conf_dw, vec(conf_dw_b),
      vec(conf_ln_g), vec(conf_ln_b), conf_pw, vec(conf_pw_b), sc_conv)


def _ffn_prompt_kernel(h_ref, g_ref, wa_ref, wb_ref, ka_ref, kb_ref, wd_ref,
                       o_ref, nfa_ref, nfb_ref,
                       hn_ref, ea_ref, eb_ref, ca_ref, cb_ref, *, tm, n_s):
    c = pl.program_id(1)
    s = pl.program_id(0) % n_s

    @pl.when(c == 0)
    def _():
        hn_ref[...] = _rms(h_ref[...], g_ref[...])

    hn = hn_ref[...]

    def conv(w_ref, k_ref, e_ref, carry_ref, nf_ref):
        @pl.when(s == 0)
        def _():
            e_ref[0:_HALO_SHORT, :] = jnp.zeros((_HALO_SHORT, e_ref.shape[1]), jnp.float32)

        @pl.when(s != 0)
        def _():
            e_ref[0:_HALO_SHORT, :] = carry_ref[c]

        e_ref[_HALO_SHORT:_HALO_SHORT + tm, :] = _dot(hn, w_ref[...])
        out = k_ref[0:1, :] * e_ref[_HALO_SHORT - 2:_HALO_SHORT - 2 + tm, :]
        out = out + k_ref[1:2, :] * e_ref[_HALO_SHORT - 1:_HALO_SHORT - 1 + tm, :]
        out = out + k_ref[2:3, :] * e_ref[_HALO_SHORT:_HALO_SHORT + tm, :]
        carry_ref[c] = e_ref[tm:tm + _HALO_SHORT, :]
        nf_ref[0] = e_ref[tm + _HALO_SHORT - (FFN_K - 1):tm + _HALO_SHORT, :]
        return out

    a = conv(wa_ref, ka_ref, ea_ref, ca_ref, nfa_ref)
    b = conv(wb_ref, kb_ref, eb_ref, cb_ref, nfb_ref)
    contrib = _dot(_silu(a) * b, wd_ref[...])

    @pl.when(c == 0)
    def _():
        o_ref[...] = h_ref[...] + contrib

    @pl.when(c != 0)
    def _():
        o_ref[...] += contrib


def _ffn_prompt(h, nb, norm_g, w_up, ffn_conv, w_down, *, tm, tf):
    m = h.shape[0]
    n_s = m // nb // tm
    n_c = D_FF // tf
    return pl.pallas_call(
        functools.partial(_ffn_prompt_kernel, tm=tm, n_s=n_s),
        out_shape=(
            jax.ShapeDtypeStruct((m, D_MODEL), jnp.float32),
            jax.ShapeDtypeStruct((m // tm, FFN_K - 1, D_FF), jnp.float32),
            jax.ShapeDtypeStruct((m // tm, FFN_K - 1, D_FF), jnp.float32),
        ),
        grid=(m // tm, n_c),
        in_specs=[
            _single((tm, D_MODEL), lambda i, c: (i, 0)),
            pl.BlockSpec((1, D_MODEL), lambda i, c: (0, 0)),
            pl.BlockSpec((D_MODEL, tf), lambda i, c: (0, c)),
            pl.BlockSpec((D_MODEL, tf), lambda i, c: (0, n_c + c)),
            pl.BlockSpec((FFN_K, tf), lambda i, c: (0, c)),
            pl.BlockSpec((FFN_K, tf), lambda i, c: (0, n_c + c)),
            pl.BlockSpec((tf, D_MODEL), lambda i, c: (c, 0)),
        ],
        out_specs=(
            pl.BlockSpec((tm, D_MODEL), lambda i, c: (i, 0)),
            pl.BlockSpec((1, FFN_K - 1, tf), lambda i, c: (i, 0, c)),
            pl.BlockSpec((1, FFN_K - 1, tf), lambda i, c: (i, 0, c)),
        ),
        scratch_shapes=[
            pltpu.VMEM((tm, D_MODEL), jnp.float32),
            pltpu.VMEM((_HALO_SHORT + tm, tf), jnp.float32),
            pltpu.VMEM((_HALO_SHORT + tm, tf), jnp.float32),
            pltpu.VMEM((n_c, _HALO_SHORT, tf), jnp.float32),
            pltpu.VMEM((n_c, _HALO_SHORT, tf), jnp.float32),
        ],
        compiler_params=_params("arbitrary", "arbitrary"),
        name="ffn_prompt",
    )(h, norm_g.reshape(1, -1), w_up, w_up, ffn_conv, ffn_conv, w_down)


def _ffn_sample_kernel(h_ref, g_ref, wa_ref, wb_ref, ka_ref, kb_ref, wd_ref,
                       s0a_ref, s1a_ref, s0b_ref, s1b_ref,
                       o_ref, ua_ref, ub_ref):
    c = pl.program_id(0)
    hn = _rms(h_ref[...], g_ref[...])
    ua = _dot(hn, wa_ref[...])
    ub = _dot(hn, wb_ref[...])
    ua_ref[...] = ua
    ub_ref[...] = ub
    a = ka_ref[0:1, :] * s0a_ref[...] + ka_ref[1:2, :] * s1a_ref[...] + ka_ref[2:3, :] * ua
    b = kb_ref[0:1, :] * s0b_ref[...] + kb_ref[1:2, :] * s1b_ref[...] + kb_ref[2:3, :] * ub
    contrib = _dot(_silu(a) * b, wd_ref[...])

    @pl.when(c == 0)
    def _():
        o_ref[...] = h_ref[...] + contrib

    @pl.when(c != 0)
    def _():
        o_ref[...] += contrib


def _ffn_sample(h, norm_g, w_up, ffn_conv, w_down, st0, st1, *, tf):
    m = h.shape[0]
    n_c = D_FF // tf
    lo = lambda c: (0, c)
    hi = lambda c: (0, n_c + c)
    return pl.pallas_call(
        _ffn_sample_kernel,
        out_shape=(
            jax.ShapeDtypeStruct((m, D_MODEL), jnp.float32),
            jax.ShapeDtypeStruct((m, D_FF), jnp.float32),
            jax.ShapeDtypeStruct((m, D_FF), jnp.float32),
        ),
        grid=(n_c,),
        in_specs=[
            pl.BlockSpec((m, D_MODEL), lambda c: (0, 0)),
            pl.BlockSpec((1, D_MODEL), lambda c: (0, 0)),
            pl.BlockSpec((D_MODEL, tf), lo),
            pl.BlockSpec((D_MODEL, tf), hi),
            pl.BlockSpec((FFN_K, tf), lo),
            pl.BlockSpec((FFN_K, tf), hi),
            pl.BlockSpec((tf, D_MODEL), lambda c: (c, 0)),
            pl.BlockSpec((m, tf), lo),
            pl.BlockSpec((m, tf), lo),
            pl.BlockSpec((m, tf), hi),
            pl.BlockSpec((m, tf), hi),
        ],
        out_specs=(
            pl.BlockSpec((m, D_MODEL), lambda c: (0, 0)),
            pl.BlockSpec((m, tf), lo),
            pl.BlockSpec((m, tf), lo),
        ),
        compiler_params=_params("arbitrary"),
        name="ffn_sample",
    )(h, norm_g.reshape(1, -1), w_up, w_up, ffn_conv, ffn_conv, w_down, st0, st1, st0, st1)


def _ple_kernel(h_ref, g_ref, gate_ref, p_ref, proj_ref, gf_ref, o_ref, *, final):
    h = h_ref[...]
    gate = jax.nn.sigmoid(_dot(_rms(h, g_ref[...]), gate_ref[...]))
    out = h + _dot(p_ref[...], proj_ref[...]) * gate
    if final:
        out = _rms(out, gf_ref[...])
    o_ref[...] = out


def _ple(h, norm_g, gate_w, p, proj_w, final_g, *, tm, final, name):
    m = h.shape[0]
    return pl.pallas_call(
        functools.partial(_ple_kernel, final=final),
        out_shape=jax.ShapeDtypeStruct((m, D_MODEL), jnp.float32),
        grid=(m // tm,),
        in_specs=[
            pl.BlockSpec((tm, D_MODEL), lambda i: (i, 0)),
            pl.BlockSpec((1, D_MODEL), lambda i: (0, 0)),
            _single((D_MODEL, D_MODEL), lambda i: (0, 0)),
            pl.BlockSpec((tm, PLE_DIM), lambda i: (i, 0)),
            _single((PLE_DIM, D_MODEL), lambda i: (0, 0)),
            pl.BlockSpec((1, D_MODEL), lambda i: (0, 0)),
        ],
        out_specs=pl.BlockSpec((tm, D_MODEL), lambda i: (i, 0)),
        compiler_params=_params("arbitrary"),
        name=name,
    )(h, norm_g.reshape(1, -1), gate_w, p, proj_w, final_g.reshape(1, -1))


def kernel(x_prompt, x_sample, p_prompt, p_sample, state_pool, state_conf, state_sc, state_ffn,
           norm_mix, w_in, w_pool, pool_scale, conf_dw, conf_dw_b, conf_ln_g, conf_ln_b,
           conf_pw, conf_pw_b, sc_conv, w_out, norm_ffn, w_up, ffn_conv, w_down,
           norm_ple, ple_gate, ple_proj, norm_final):
    nb, seq, _ = x_prompt.shape
    ns = x_sample.shape[0]
    depth = w_in.shape[0]
    hp = x_prompt.reshape(nb * seq, D_MODEL)
    hs = x_sample.reshape(ns, D_MODEL)
    tn_in = IN_COLS // 2

    outs = [[] for _ in range(8)]
    for i in range(depth):
        last = i == depth - 1
        mix_w = (w_pool[i], pool_scale[i], conf_dw[i], conf_dw_b[i], conf_ln_g[i], conf_ln_b[i],
                 conf_pw[i], conf_pw_b[i], sc_conv[i])

        z = _norm_matmul(hp, norm_mix[i], w_in[i], tm=512, tn=tn_in, name="in_proj_prompt")
        ycat, nconf_p, nsc_p = _mixer_prompt(z, nb, *mix_w, tm=256)
        npool_p = z.reshape(nb, seq, IN_COLS)[:, seq - POOL_BUF:, :POOL_WIDTH]
        h1 = _matmul_res(ycat, w_out[i], hp, tm=512, name="out_proj_prompt")
        h2, nfa, nfb = _ffn_prompt(h1, nb, norm_ffn[i], w_up[i], ffn_conv[i], w_down[i],
                                   tm=1024, tf=256)
        tiles_per_seq = nfa.shape[0] // nb
        nffn_p = jnp.concatenate([nfa, nfb], axis=-1)[tiles_per_seq - 1::tiles_per_seq]
        hp = _ple(h2, norm_ple[i], ple_gate[i], p_prompt[i].reshape(nb * seq, PLE_DIM),
                  ple_proj[i], norm_final, tm=512, final=last, name="ple_prompt")

        zs = _norm_matmul(hs, norm_mix[i], w_in[i], tm=ns, tn=tn_in, name="in_proj_sample")
        ycat_s, g_s, v_s = _mixer_sample(
            zs, jnp.swapaxes(state_pool[i], 0, 1), jnp.swapaxes(state_conf[i], 0, 1),
            jnp.swapaxes(state_sc[i], 0, 1), *mix_w)
        h1s = _matmul_res(ycat_s, w_out[i], hs, tm=ns, name="out_proj_sample")
        h2s, ua_s, ub_s = _ffn_sample(h1s, norm_ffn[i], w_up[i], ffn_conv[i], w_down[i],
                                      state_ffn[i][:, 0, :], state_ffn[i][:, 1, :], tf=512)
        hs = _ple(h2s, norm_ple[i], ple_gate[i], p_sample[i].reshape(ns, PLE_DIM),
                  ple_proj[i], norm_final, tm=ns, final=last, name="ple_sample")

        npool_s = jnp.concatenate([state_pool[i][:, 1:], zs[:, None, :POOL_WIDTH]], axis=1)
        nconf_s = jnp.concatenate([state_conf[i][:, 1:], g_s[:, None]], axis=1)
        nsc_s = jnp.concatenate([state_sc[i][:, 1:], v_s[:, None]], axis=1)
        up_s = jnp.concatenate([ua_s, ub_s], axis=-1)
        nffn_s = jnp.concatenate([state_ffn[i][:, 1:], up_s[:, None]], axis=1)

        for lst, val in zip(outs, (npool_p, npool_s, nconf_p, nconf_s, nsc_p, nsc_s, nffn_p, nffn_s)):
            lst.append(val)

    y_prompt = hp.reshape(nb, seq, D_MODEL)
    y_sample = hs.reshape(ns, 1, D_MODEL)
    return (y_prompt, y_sample) + tuple(jnp.stack(l) for l in outs)
```

```python
import functools

import jax
import jax.numpy as jnp
from jax import lax
from jax.experimental import pallas as pl
from jax.experimental.pallas import tpu as pltpu

D_MODEL = 2048
POOL_WIDTH = 512
POOL_WINDOWS = (2, 4, 8, 16)
POOL_GW = 128
POOL_BUF = 15
CONV_WIDTH = 768
CONF_K = 31
SC_WIDTH = 768
SC_K = 3
FFN_K = 3
D_FF = 5632
PLE_DIM = 256
IN_COLS = POOL_WIDTH + 2 * CONV_WIDTH + 3 * SC_WIDTH
EPS = 1e-6
PAST_LEN = 16384

_O_U = 0
_O_GA = _O_U + POOL_WIDTH
_O_GB = _O_GA + CONV_WIDTH
_O_SB = _O_GB + CONV_WIDTH
_O_SC = _O_SB + SC_WIDTH
_O_SH = _O_SC + SC_WIDTH

_VMEM_LIMIT = 56 * 1024 * 1024
_LANE = 128

_HALO_POOL = 16
_HALO_CONF = 32
_HALO_SHORT = 8

_TM_IN, _TN_IN = 512, IN_COLS // 2
_TM_MIX = 256
_TM_OUT = 512
_TM_FFN, _TF_FFN = 1024, 256
_TF_FFN_SAMPLE = 512
_TM_PLE = 512

_BF16 = jnp.bfloat16
_F32 = jnp.float32


def _params(*sem):
    return pltpu.CompilerParams(dimension_semantics=sem, vmem_limit_bytes=_VMEM_LIMIT)


def _rms(x, g):
    ms = jnp.mean(x * x, axis=-1, keepdims=True)
    return x * lax.rsqrt(ms + EPS) * g


def _dot(a, b):
    return jnp.dot(a, b, preferred_element_type=_F32)


def _silu(x):
    return x * jax.nn.sigmoid(x)


def _single(block, index_map):
    return pl.BlockSpec(block, index_map, pipeline_mode=pl.Buffered(1))


def _vec3(a):
    return a.reshape(a.shape[0], 1, a.shape[1])


def _norm_matmul_kernel(x_ref, g_ref, w_ref, o_ref):
    o_ref[...] = _dot(_rms(x_ref[...], g_ref[...]), w_ref[...])


def _norm_matmul(x, g, w, layer, *, tm, tn, name):
    m, k = x.shape
    n = w.shape[2]
    return pl.pallas_call(
        _norm_matmul_kernel,
        out_shape=jax.ShapeDtypeStruct((m, n), _F32),
        grid=(n // tn, m // tm),
        in_specs=[
            pl.BlockSpec((tm, k), lambda j, i: (i, 0)),
            pl.BlockSpec((None, 1, k), lambda j, i: (layer, 0, 0)),
            _single((None, k, tn), lambda j, i: (layer, 0, j)),
        ],
        out_specs=pl.BlockSpec((tm, tn), lambda j, i: (i, j)),
        compiler_params=_params("arbitrary", "arbitrary"),
        name=name,
    )(x, _vec3(g), w)


def _matmul_res_kernel(x_ref, w_ref, r_ref, o_ref):
    o_ref[...] = r_ref[...] + _dot(x_ref[...], w_ref[...])


def _matmul_res(x, w, layer, res, *, tm, name):
    m, k = x.shape
    n = w.shape[2]
    return pl.pallas_call(
        _matmul_res_kernel,
        out_shape=jax.ShapeDtypeStruct((m, n), _F32),
        grid=(m // tm,),
        in_specs=[
            pl.BlockSpec((tm, k), lambda i: (i, 0)),
            _single((None, k, n), lambda i: (layer, 0, 0)),
            pl.BlockSpec((tm, n), lambda i: (i, 0)),
        ],
        out_specs=pl.BlockSpec((tm, n), lambda i: (i, 0)),
        compiler_params=_params("arbitrary"),
        name=name,
    )(x, w, res)


def _mixer_weight_specs(layer, idx):
    l3 = lambda *_: (layer, 0, 0)
    l4 = lambda *_: (layer, 0, 0, 0)
    del idx
    return [
        pl.BlockSpec((None, len(POOL_WINDOWS), POOL_GW, POOL_GW), l4),
        pl.BlockSpec((None, 1, POOL_WIDTH), l3),
        pl.BlockSpec((None, CONF_K, CONV_WIDTH), l3),
        pl.BlockSpec((None, 1, CONV_WIDTH), l3),
        pl.BlockSpec((None, 1, CONV_WIDTH), l3),
        pl.BlockSpec((None, 1, CONV_WIDTH), l3),
        pl.BlockSpec((None, CONV_WIDTH, CONV_WIDTH), l3),
        pl.BlockSpec((None, 1, CONV_WIDTH), l3),
        pl.BlockSpec((None, SC_K, SC_WIDTH), l3),
    ]


def _mixer_weight_args(w_pool, pool_scale, conf_dw, conf_dw_b, conf_ln_g, conf_ln_b,
                       conf_pw, conf_pw_b, sc_conv):
    return (w_pool, _vec3(pool_scale), conf_dw, _vec3(conf_dw_b), _vec3(conf_ln_g),
            _vec3(conf_ln_b), conf_pw, _vec3(conf_pw_b), sc_conv)


def _layernorm_silu_proj(cb, lng_ref, lnb_ref, pw_ref, pwb_ref):
    mu = jnp.mean(cb, axis=-1, keepdims=True)
    xc = cb - mu
    var = jnp.mean(xc * xc, axis=-1, keepdims=True)
    ln = xc * lax.rsqrt(var + EPS) * lng_ref[...] + lnb_ref[...]
    return _dot(_silu(ln), pw_ref[...]) + pwb_ref[...]


def _mixer_prompt_kernel(z_ref, wpool_ref, pscale_ref, dw_ref, dwb_ref, lng_ref, lnb_ref,
                         pw_ref, pwb_ref, scw_ref,
                         y_ref, nconf_ref, nsc_ref,
                         eu_ref, eg_ref, ev_ref, *, tm, n_s):
    s = pl.program_id(1)

    @pl.when(s == 0)
    def _():
        eu_ref[0:_HALO_POOL, :] = jnp.zeros((_HALO_POOL, POOL_WIDTH), _F32)
        eg_ref[0:_HALO_CONF, :] = jnp.zeros((_HALO_CONF, CONV_WIDTH), _F32)
        ev_ref[0:_HALO_SHORT, :] = jnp.zeros((_HALO_SHORT, SC_WIDTH), _F32)

    eu_ref[_HALO_POOL:_HALO_POOL + tm, :] = z_ref[:, _O_U:_O_U + POOL_WIDTH]
    pos = s * tm + lax.broadcasted_iota(jnp.int32, (tm, 1), 0)
    for g, w in enumerate(POOL_WINDOWS):
        c0 = g * POOL_GW
        u = eu_ref[_HALO_POOL:_HALO_POOL + tm, c0:c0 + POOL_GW]
        wsum = u
        for k in range(1, w):
            wsum = wsum + eu_ref[_HALO_POOL - k:_HALO_POOL - k + tm, c0:c0 + POOL_GW]
        cnt = jnp.minimum(w, pos + 1).astype(_F32)
        d = wsum / cnt - u
        y_ref[:, c0:c0 + POOL_GW] = _dot(d, wpool_ref[g]) * pscale_ref[:, c0:c0 + POOL_GW]
    eu_ref[0:_HALO_POOL, :] = eu_ref[tm:tm + _HALO_POOL, :]

    eg_ref[_HALO_CONF:_HALO_CONF + tm, :] = (
        z_ref[:, _O_GA:_O_GA + CONV_WIDTH] * jax.nn.sigmoid(z_ref[:, _O_GB:_O_GB + CONV_WIDTH]))
    base = _HALO_CONF - (CONF_K - 1)
    for c0 in range(0, CONV_WIDTH, _LANE):
        acc = dw_ref[0:1, c0:c0 + _LANE] * eg_ref[base:base + tm, c0:c0 + _LANE]
        for k in range(1, CONF_K):
            acc = acc + dw_ref[k:k + 1, c0:c0 + _LANE] * eg_ref[base + k:base + k + tm, c0:c0 + _LANE]
        ev_ref[_HALO_SHORT:_HALO_SHORT + tm, c0:c0 + _LANE] = acc + dwb_ref[:, c0:c0 + _LANE]
    y_ref[:, POOL_WIDTH:POOL_WIDTH + CONV_WIDTH] = _layernorm_silu_proj(
        ev_ref[_HALO_SHORT:_HALO_SHORT + tm, :], lng_ref, lnb_ref, pw_ref, pwb_ref)

    @pl.when(s == n_s - 1)
    def _():
        nconf_ref[...] = eg_ref[tm + base:tm + _HALO_CONF, :]

    eg_ref[0:_HALO_CONF, :] = eg_ref[tm:tm + _HALO_CONF, :]

    ev_ref[_HALO_SHORT:_HALO_SHORT + tm, :] = (
        z_ref[:, _O_SC:_O_SC + SC_WIDTH] * z_ref[:, _O_SH:_O_SH + SC_WIDTH])
    cv = scw_ref[0:1, :] * ev_ref[_HALO_SHORT - 2:_HALO_SHORT - 2 + tm, :]
    cv = cv + scw_ref[1:2, :] * ev_ref[_HALO_SHORT - 1:_HALO_SHORT - 1 + tm, :]
    cv = cv + scw_ref[2:3, :] * ev_ref[_HALO_SHORT:_HALO_SHORT + tm, :]
    y_ref[:, POOL_WIDTH + CONV_WIDTH:D_MODEL] = z_ref[:, _O_SB:_O_SB + SC_WIDTH] * cv

    @pl.when(s == n_s - 1)
    def _():
        nsc_ref[...] = ev_ref[tm + _HALO_SHORT - (SC_K - 1):tm + _HALO_SHORT, :]

    ev_ref[0:_HALO_SHORT, :] = ev_ref[tm:tm + _HALO_SHORT, :]


def _mixer_prompt(z, nb, layer, mix_w, *, tm):
    m = z.shape[0]
    n_s = m // nb // tm
    row = lambda b, s: (b * n_s + s, 0)
    return pl.pallas_call(
        functools.partial(_mixer_prompt_kernel, tm=tm, n_s=n_s),
        out_shape=(
            jax.ShapeDtypeStruct((m, D_MODEL), _F32),
            jax.ShapeDtypeStruct((nb, CONF_K - 1, CONV_WIDTH), _F32),
            jax.ShapeDtypeStruct((nb, SC_K - 1, SC_WIDTH), _F32),
        ),
        grid=(nb, n_s),
        in_specs=[pl.BlockSpec((tm, IN_COLS), row)] + _mixer_weight_specs(layer, 2),
        out_specs=(
            pl.BlockSpec((tm, D_MODEL), row),
            pl.BlockSpec((None, CONF_K - 1, CONV_WIDTH), lambda b, s: (b, 0, 0)),
            pl.BlockSpec((None, SC_K - 1, SC_WIDTH), lambda b, s: (b, 0, 0)),
        ),
        scratch_shapes=[
            pltpu.VMEM((_HALO_POOL + tm, POOL_WIDTH), _F32),
            pltpu.VMEM((_HALO_CONF + tm, CONV_WIDTH), _F32),
            pltpu.VMEM((_HALO_SHORT + tm, SC_WIDTH), _F32),
        ],
        compiler_params=_params("arbitrary", "arbitrary"),
        name="mixer_prompt",
    )(z, *_mixer_weight_args(*mix_w))


def _mixer_sample_kernel(z_ref, spool_ref, sconf_ref, ssc_ref,
                         wpool_ref, pscale_ref, dw_ref, dwb_ref, lng_ref, lnb_ref,
                         pw_ref, pwb_ref, scw_ref,
                         y_ref, g_ref, v_ref):
    for g, w in enumerate(POOL_WINDOWS):
        c0 = g * POOL_GW
        u = z_ref[:, _O_U + c0:_O_U + c0 + POOL_GW]
        wsum = u
        for k in range(1, w):
            wsum = wsum + spool_ref[POOL_BUF - k, :, c0:c0 + POOL_GW]
        d = wsum / jnp.float32(min(w, PAST_LEN + 1)) - u
        y_ref[:, c0:c0 + POOL_GW] = _dot(d, wpool_ref[g]) * pscale_ref[:, c0:c0 + POOL_GW]

    gl = z_ref[:, _O_GA:_O_GA + CONV_WIDTH] * jax.nn.sigmoid(z_ref[:, _O_GB:_O_GB + CONV_WIDTH])
    g_ref[...] = gl
    acc = dw_ref[CONF_K - 1:CONF_K, :] * gl
    for k in range(CONF_K - 1):
        acc = acc + dw_ref[k:k + 1, :] * sconf_ref[k]
    y_ref[:, POOL_WIDTH:POOL_WIDTH + CONV_WIDTH] = _layernorm_silu_proj(
        acc + dwb_ref[...], lng_ref, lnb_ref, pw_ref, pwb_ref)

    v = z_ref[:, _O_SC:_O_SC + SC_WIDTH] * z_ref[:, _O_SH:_O_SH + SC_WIDTH]
    v_ref[...] = v
    cv = scw_ref[0:1, :] * ssc_ref[0] + scw_ref[1:2, :] * ssc_ref[1] + scw_ref[2:3, :] * v
    y_ref[:, POOL_WIDTH + CONV_WIDTH:D_MODEL] = z_ref[:, _O_SB:_O_SB + SC_WIDTH] * cv


def _mixer_sample(z, spool_t, sconf_t, ssc_t, layer, mix_w):
    m = z.shape[0]
    full = lambda a: pl.BlockSpec(a.shape, lambda i: (0,) * a.ndim)
    return pl.pallas_call(
        _mixer_sample_kernel,
        out_shape=(
            jax.ShapeDtypeStruct((m, D_MODEL), _F32),
            jax.ShapeDtypeStruct((m, CONV_WIDTH), _F32),
            jax.ShapeDtypeStruct((m, SC_WIDTH), _F32),
        ),
        grid=(1,),
        in_specs=[full(z), full(spool_t), full(sconf_t), full(ssc_t)] + _mixer_weight_specs(layer, 1),
        out_specs=(
            pl.BlockSpec((m, D_MODEL), lambda i: (0, 0)),
            pl.BlockSpec((m, CONV_WIDTH), lambda i: (0, 0)),
            pl.BlockSpec((m, SC_WIDTH), lambda i: (0, 0)),
        ),
        compiler_params=_params("arbitrary"),
        name="mixer_sample",
    )(z, spool_t, sconf_t, ssc_t, *_mixer_weight_args(*mix_w))


def _ffn_prompt_kernel(h_ref, g_ref, wa_ref, wb_ref, ka_ref, kb_ref, wd_ref,
                       o_ref, nfa_ref, nfb_ref,
                       hn_ref, ea_ref, eb_ref, ca_ref, cb_ref, *, tm, n_s):
    c = pl.program_id(1)
    s = pl.program_id(0) % n_s

    @pl.when(c == 0)
    def _():
        h = h_ref[...]
        hn_ref[...] = _rms(h, g_ref[...]).astype(_BF16)
        o_ref[...] = h

    @pl.when(s == 0)
    def _():
        ca_ref[c] = jnp.zeros(ca_ref.shape[1:], _F32)
        cb_ref[c] = jnp.zeros(cb_ref.shape[1:], _F32)

    hn = hn_ref[...]

    def conv(w_ref, k_ref, e_ref, carry_ref, nf_ref):
        e_ref[0:_HALO_SHORT, :] = carry_ref[c]
        e_ref[_HALO_SHORT:_HALO_SHORT + tm, :] = _dot(hn, w_ref[...].astype(_BF16))
        out = k_ref[0:1, :] * e_ref[_HALO_SHORT - 2:_HALO_SHORT - 2 + tm, :]
        out = out + k_ref[1:2, :] * e_ref[_HALO_SHORT - 1:_HALO_SHORT - 1 + tm, :]
        out = out + k_ref[2:3, :] * e_ref[_HALO_SHORT:_HALO_SHORT + tm, :]
        carry_ref[c] = e_ref[tm:tm + _HALO_SHORT, :]
        nf_ref[...] = e_ref[tm + _HALO_SHORT - (FFN_K - 1):tm + _HALO_SHORT, :]
        return out

    a = conv(wa_ref, ka_ref, ea_ref, ca_ref, nfa_ref)
    b = conv(wb_ref, kb_ref, eb_ref, cb_ref, nfb_ref)
    act = (_silu(a) * b).astype(_BF16)
    o_ref[...] += _dot(act, wd_ref[...].astype(_BF16))


def _ffn_prompt(h, nb, norm_g, w_up, ffn_conv, w_down, layer, *, tm, tf):
    m = h.shape[0]
    n_s = m // nb // tm
    n_c = D_FF // tf
    return pl.pallas_call(
        functools.partial(_ffn_prompt_kernel, tm=tm, n_s=n_s),
        out_shape=(
            jax.ShapeDtypeStruct((m, D_MODEL), _F32),
            jax.ShapeDtypeStruct((m // tm, FFN_K - 1, D_FF), _F32),
            jax.ShapeDtypeStruct((m // tm, FFN_K - 1, D_FF), _F32),
        ),
        grid=(m // tm, n_c),
        in_specs=[
            _single((tm, D_MODEL), lambda i, c: (i, 0)),
            pl.BlockSpec((None, 1, D_MODEL), lambda i, c: (layer, 0, 0)),
            pl.BlockSpec((None, D_MODEL, tf), lambda i, c: (layer, 0, c)),
            pl.BlockSpec((None, D_MODEL, tf), lambda i, c: (layer, 0, n_c + c)),
            pl.BlockSpec((None, FFN_K, tf), lambda i, c: (layer, 0, c)),
            pl.BlockSpec((None, FFN_K, tf), lambda i, c: (layer, 0, n_c + c)),
            pl.BlockSpec((None, tf, D_MODEL), lambda i, c: (layer, c, 0)),
        ],
        out_specs=(
            pl.BlockSpec((tm, D_MODEL), lambda i, c: (i, 0)),
            pl.BlockSpec((None, FFN_K - 1, tf), lambda i, c: (i, 0, c)),
            pl.BlockSpec((None, FFN_K - 1, tf), lambda i, c: (i, 0, c)),
        ),
        scratch_shapes=[
            pltpu.VMEM((tm, D_MODEL), _BF16),
            pltpu.VMEM((_HALO_SHORT + tm, tf), _F32),
            pltpu.VMEM((_HALO_SHORT + tm, tf), _F32),
            pltpu.VMEM((n_c, _HALO_SHORT, tf), _F32),
            pltpu.VMEM((n_c, _HALO_SHORT, tf), _F32),
        ],
        compiler_params=_params("arbitrary", "arbitrary"),
        name="ffn_prompt",
    )(h, _vec3(norm_g), w_up, w_up, ffn_conv, ffn_conv, w_down)


def _ffn_sample_kernel(h_ref, g_ref, wa_ref, wb_ref, ka_ref, kb_ref, wd_ref,
                       s0a_ref, s1a_ref, s0b_ref, s1b_ref,
                       o_ref, ua_ref, ub_ref):
    c = pl.program_id(0)

    @pl.when(c == 0)
    def _():
        o_ref[...] = h_ref[...]

    hn = _rms(h_ref[...], g_ref[...])
    ua = _dot(hn, wa_ref[...])
    ub = _dot(hn, wb_ref[...])
    ua_ref[...] = ua
    ub_ref[...] = ub
    a = ka_ref[0:1, :] * s0a_ref[...] + ka_ref[1:2, :] * s1a_ref[...] + ka_ref[2:3, :] * ua
    b = kb_ref[0:1, :] * s0b_ref[...] + kb_ref[1:2, :] * s1b_ref[...] + kb_ref[2:3, :] * ub
    o_ref[...] += _dot(_silu(a) * b, wd_ref[...])


def _ffn_sample(h, norm_g, w_up, ffn_conv, w_down, layer, st0, st1, *, tf):
    m = h.shape[0]
    n_c = D_FF // tf
    lo = lambda c: (0, c)
    hi = lambda c: (0, n_c + c)
    return pl.pallas_call(
        _ffn_sample_kernel,
        out_shape=(
            jax.ShapeDtypeStruct((m, D_MODEL), _F32),
            jax.ShapeDtypeStruct((m, D_FF), _F32),
            jax.ShapeDtypeStruct((m, D_FF), _F32),
        ),
        grid=(n_c,),
        in_specs=[
            pl.BlockSpec((m, D_MODEL), lambda c: (0, 0)),
            pl.BlockSpec((None, 1, D_MODEL), lambda c: (layer, 0, 0)),
            pl.BlockSpec((None, D_MODEL, tf), lambda c: (layer, 0, c)),
            pl.BlockSpec((None, D_MODEL, tf), lambda c: (layer, 0, n_c + c)),
            pl.BlockSpec((None, FFN_K, tf), lambda c: (layer, 0, c)),
            pl.BlockSpec((None, FFN_K, tf), lambda c: (layer, 0, n_c + c)),
            pl.BlockSpec((None, tf, D_MODEL), lambda c: (layer, c, 0)),
            pl.BlockSpec((m, tf), lo),
            pl.BlockSpec((m, tf), lo),
            pl.BlockSpec((m, tf), hi),
            pl.BlockSpec((m, tf), hi),
        ],
        out_specs=(
            pl.BlockSpec((m, D_MODEL), lambda c: (0, 0)),
            pl.BlockSpec((m, tf), lo),
            pl.BlockSpec((m, tf), lo),
        ),
        compiler_params=_params("arbitrary"),
        name="ffn_sample",
    )(h, _vec3(norm_g), w_up, w_up, ffn_conv, ffn_conv, w_down, st0, st1, st0, st1)


def _ple_kernel(h_ref, g_ref, gate_ref, p_ref, proj_ref, gf_ref, o_ref, *, final):
    h = h_ref[...]
    gate = jax.nn.sigmoid(_dot(_rms(h, g_ref[...]), gate_ref[...]))
    out = h + _dot(p_ref[...], proj_ref[...]) * gate
    if final:
        out = _rms(out, gf_ref[...])
    o_ref[...] = out


def _ple(h, norm_g, gate_w, p, proj_w, layer, final_g, *, tm, final, name):
    m = h.shape[0]
    return pl.pallas_call(
        functools.partial(_ple_kernel, final=final),
        out_shape=jax.ShapeDtypeStruct((m, D_MODEL), _F32),
        grid=(m // tm,),
        in_specs=[
            pl.BlockSpec((tm, D_MODEL), lambda i: (i, 0)),
            pl.BlockSpec((None, 1, D_MODEL), lambda i: (layer, 0, 0)),
            _single((None, D_MODEL, D_MODEL), lambda i: (layer, 0, 0)),
            pl.BlockSpec((None, tm, PLE_DIM), lambda i: (layer, i, 0)),
            _single((None, PLE_DIM, D_MODEL), lambda i: (layer, 0, 0)),
            pl.BlockSpec((1, D_MODEL), lambda i: (0, 0)),
        ],
        out_specs=pl.BlockSpec((tm, D_MODEL), lambda i: (i, 0)),
        compiler_params=_params("arbitrary"),
        name=name,
    )(h, _vec3(norm_g), gate_w, p, proj_w, final_g.reshape(1, -1))


def kernel(x_prompt, x_sample, p_prompt, p_sample, state_pool, state_conf, state_sc, state_ffn,
           norm_mix, w_in, w_pool, pool_scale, conf_dw, conf_dw_b, conf_ln_g, conf_ln_b,
           conf_pw, conf_pw_b, sc_conv, w_out, norm_ffn, w_up, ffn_conv, w_down,
           norm_ple, ple_gate, ple_proj, norm_final):
    nb, seq, _ = x_prompt.shape
    ns = x_sample.shape[0]
    depth = w_in.shape[0]
    hp = x_prompt.reshape(nb * seq, D_MODEL)
    hs = x_sample.reshape(ns, D_MODEL)
    pp = p_prompt.reshape(depth, nb * seq, PLE_DIM)
    ps = p_sample.reshape(depth, ns, PLE_DIM)
    mix_w = (w_pool, pool_scale, conf_dw, conf_dw_b, conf_ln_g, conf_ln_b, conf_pw, conf_pw_b, sc_conv)

    outs = [[] for _ in range(8)]
    for i in range(depth):
        last = i == depth - 1

        z = _norm_matmul(hp, norm_mix, w_in, i, tm=_TM_IN, tn=_TN_IN, name="in_proj_prompt")
        ycat, nconf_p, nsc_p = _mixer_prompt(z, nb, i, mix_w, tm=_TM_MIX)
        npool_p = z.reshape(nb, seq, IN_COLS)[:, seq - POOL_BUF:, :POOL_WIDTH]
        h1 = _matmul_res(ycat, w_out, i, hp, tm=_TM_OUT, name="out_proj_prompt")
        h2, nfa, nfb = _ffn_prompt(h1, nb, norm_ffn, w_up, ffn_conv, w_down, i,
                                   tm=_TM_FFN, tf=_TF_FFN)
        tiles_per_seq = nfa.shape[0] // nb
        nffn_p = jnp.concatenate([nfa, nfb], axis=-1)[tiles_per_seq - 1::tiles_per_seq]
        hp = _ple(h2, norm_ple, ple_gate, pp, ple_proj, i, norm_final,
                  tm=_TM_PLE, final=last, name="ple_prompt")

        zs = _norm_matmul(hs, norm_mix, w_in, i, tm=ns, tn=_TN_IN, name="in_proj_sample")
        ycat_s, g_s, v_s = _mixer_sample(
            zs, jnp.swapaxes(state_pool[i], 0, 1), jnp.swapaxes(state_conf[i], 0, 1),
            jnp.swapaxes(state_sc[i], 0, 1), i, mix_w)
        h1s = _matmul_res(ycat_s, w_out, i, hs, tm=ns, name="out_proj_sample")
        h2s, ua_s, ub_s = _ffn_sample(h1s, norm_ffn, w_up, ffn_conv, w_down, i,
                                      state_ffn[i][:, 0, :], state_ffn[i][:, 1, :],
                                      tf=_TF_FFN_SAMPLE)
        hs = _ple(h2s, norm_ple, ple_gate, ps, ple_proj, i, norm_final,
                  tm=ns, final=last, name="ple_sample")

        npool_s = jnp.concatenate([state_pool[i][:, 1:], zs[:, None, :POOL_WIDTH]], axis=1)
        nconf_s = jnp.concatenate([state_conf[i][:, 1:], g_s[:, None]], axis=1)
        nsc_s = jnp.concatenate([state_sc[i][:, 1:], v_s[:, None]], axis=1)
        up_s = jnp.concatenate([ua_s, ub_s], axis=-1)
        nffn_s = jnp.concatenate([state_ffn[i][:, 1:], up_s[:, None]], axis=1)

        for lst, val in zip(outs, (npool_p, npool_s, nconf_p, nconf_s, nsc_p, nsc_s, nffn_p, nffn_s)):
            lst.append(val)

    y_prompt = hp.reshape(nb, seq, D_MODEL)
    y_sample = hs.reshape(ns, 1, D_MODEL)
    return (y_prompt, y_sample) + tuple(jnp.stack(l) for l in outs)
```

```python
import functools

import jax
import jax.numpy as jnp
from jax import lax
from jax.experimental import pallas as pl
from jax.experimental.pallas import tpu as pltpu

D_MODEL = 2048
POOL_WIDTH = 512
POOL_WINDOWS = (2, 4, 8, 16)
POOL_GW = 128
POOL_BUF = 15
CONV_WIDTH = 768
CONF_K = 31
SC_WIDTH = 768
SC_K = 3
FFN_K = 3
D_FF = 5632
PLE_DIM = 256
IN_COLS = POOL_WIDTH + 2 * CONV_WIDTH + 3 * SC_WIDTH
EPS = 1e-6
PAST_LEN = 16384

_O_U = 0
_O_GA = _O_U + POOL_WIDTH
_O_GB = _O_GA + CONV_WIDTH
_O_SB = _O_GB + CONV_WIDTH
_O_SC = _O_SB + SC_WIDTH
_O_SH = _O_SC + SC_WIDTH
_Y_A = 0
_Y_B = POOL_WIDTH
_Y_C = POOL_WIDTH + CONV_WIDTH

_VMEM_LIMIT = 56 * 1024 * 1024
_LANE = 128
_SUBLANE = 8

_HALO_POOL = 16
_HALO_CONF = 32
_HALO_SHORT = 8

_TM_IN, _TN_IN, _SUB_IN = 512, IN_COLS // 2, 256
_TM_MIX, _SUB_MIX = 256, 128
_TM_FFN, _TF_FFN, _SUB_FFN = 1024, 256, 512
_TF_FFN_SAMPLE = 512
_TM_PLE, _SUB_PLE = 512, 256

_BF16 = jnp.bfloat16
_F32 = jnp.float32


def _params(*sem):
    return pltpu.CompilerParams(dimension_semantics=sem, vmem_limit_bytes=_VMEM_LIMIT)


def _rms(x, g):
    ms = jnp.mean(x * x, axis=-1, keepdims=True)
    return x * lax.rsqrt(ms + EPS) * g


def _dot(a, b):
    return jnp.dot(a, b, preferred_element_type=_F32)


def _silu(x):
    return x * jax.nn.sigmoid(x)


def _single(block, index_map):
    return pl.BlockSpec(block, index_map, pipeline_mode=pl.Buffered(1))


def _vec3(a):
    return a.reshape(a.shape[0], 1, a.shape[1])


def _norm_matmul_kernel(x_ref, g_ref, w_ref, o_ref, wb_ref, *, tm, sub):
    @pl.when(pl.program_id(1) == 0)
    def _():
        wb_ref[...] = w_ref[...].astype(_BF16)

    for r0 in range(0, tm, sub):
        hn = _rms(x_ref[r0:r0 + sub, :], g_ref[...]).astype(_BF16)
        o_ref[r0:r0 + sub, :] = _dot(hn, wb_ref[...])


def _norm_matmul(x, g, w, layer, *, tm, tn, sub, name):
    m, k = x.shape
    n = w.shape[2]
    return pl.pallas_call(
        functools.partial(_norm_matmul_kernel, tm=tm, sub=sub),
        out_shape=jax.ShapeDtypeStruct((m, n), _F32),
        grid=(n // tn, m // tm),
        in_specs=[
            pl.BlockSpec((tm, k), lambda j, i: (i, 0)),
            pl.BlockSpec((None, 1, k), lambda j, i: (layer, 0, 0)),
            _single((None, k, tn), lambda j, i: (layer, 0, j)),
        ],
        out_specs=pl.BlockSpec((tm, tn), lambda j, i: (i, j)),
        scratch_shapes=[pltpu.VMEM((k, tn), _BF16)],
        compiler_params=_params("arbitrary", "arbitrary"),
        name=name,
    )(x, _vec3(g), w)


def _mixer_weight_specs(layer):
    l3 = lambda *_: (layer, 0, 0)
    l4 = lambda *_: (layer, 0, 0, 0)
    return [
        _single((None, len(POOL_WINDOWS), POOL_GW, POOL_GW), l4),
        pl.BlockSpec((None, 1, POOL_WIDTH), l3),
        pl.BlockSpec((None, CONF_K, CONV_WIDTH), l3),
        pl.BlockSpec((None, 1, CONV_WIDTH), l3),
        pl.BlockSpec((None, 1, CONV_WIDTH), l3),
        pl.BlockSpec((None, 1, CONV_WIDTH), l3),
        _single((None, CONV_WIDTH, CONV_WIDTH), l3),
        pl.BlockSpec((None, 1, CONV_WIDTH), l3),
        pl.BlockSpec((None, SC_K, SC_WIDTH), l3),
    ]


def _mixer_weight_args(w_pool, pool_scale, conf_dw, conf_dw_b, conf_ln_g, conf_ln_b,
                       conf_pw, conf_pw_b, sc_conv):
    return (w_pool, _vec3(pool_scale), conf_dw, _vec3(conf_dw_b), _vec3(conf_ln_g),
            _vec3(conf_ln_b), conf_pw, _vec3(conf_pw_b), sc_conv)


def _layernorm_silu(cb, lng_ref, lnb_ref):
    mu = jnp.mean(cb, axis=-1, keepdims=True)
    xc = cb - mu
    var = jnp.mean(xc * xc, axis=-1, keepdims=True)
    return _silu(xc * lax.rsqrt(var + EPS) * lng_ref[...] + lnb_ref[...])


def _conv31_block(eg_ref, dw_ref, c0, t0, rows):
    base = _HALO_CONF - (CONF_K - 1)
    acc = None
    for r in range(_SUBLANE):
        p = None
        for q in range((CONF_K + base) // _SUBLANE + 1):
            k = _SUBLANE * q + r - base
            if 0 <= k < CONF_K:
                lo = t0 + _SUBLANE * q
                term = dw_ref[k:k + 1, c0:c0 + _LANE] * eg_ref[lo:lo + rows + _SUBLANE, c0:c0 + _LANE]
                p = term if p is None else p + term
        shifted = p[r:r + rows, :]
        acc = shifted if acc is None else acc + shifted
    return acc


def _mix_out_prompt_kernel(z_ref, h_ref, wpool_ref, pscale_ref, dw_ref, dwb_ref, lng_ref, lnb_ref,
                           pw_ref, pwb_ref, scw_ref, wout_ref,
                           o_ref, nconf_ref, nsc_ref,
                           eu_ref, eg_ref, ev_ref, cb_ref, ya_ref, wo_ref, pwbf_ref, *, tm, sub, n_s):
    b = pl.program_id(0)
    s = pl.program_id(1)

    @pl.when((b == 0) & (s == 0))
    def _():
        wo_ref[...] = wout_ref[...].astype(_BF16)
        pwbf_ref[...] = pw_ref[...].astype(_BF16)
        eg_ref[_HALO_CONF + tm:_HALO_CONF + tm + _SUBLANE, :] = jnp.zeros((_SUBLANE, CONV_WIDTH), _F32)

    @pl.when(s == 0)
    def _():
        eu_ref[0:_HALO_POOL, :] = jnp.zeros((_HALO_POOL, POOL_WIDTH), _F32)
        eg_ref[0:_HALO_CONF, :] = jnp.zeros((_HALO_CONF, CONV_WIDTH), _F32)
        ev_ref[0:_HALO_SHORT, :] = jnp.zeros((_HALO_SHORT, SC_WIDTH), _F32)

    ev_ref[_HALO_SHORT:_HALO_SHORT + tm, :] = (
        z_ref[:, _O_SC:_O_SC + SC_WIDTH] * z_ref[:, _O_SH:_O_SH + SC_WIDTH])
    cv = scw_ref[0:1, :] * ev_ref[_HALO_SHORT - 2:_HALO_SHORT - 2 + tm, :]
    cv = cv + scw_ref[1:2, :] * ev_ref[_HALO_SHORT - 1:_HALO_SHORT - 1 + tm, :]
    cv = cv + scw_ref[2:3, :] * ev_ref[_HALO_SHORT:_HALO_SHORT + tm, :]
    y_c = (z_ref[:, _O_SB:_O_SB + SC_WIDTH] * cv).astype(_BF16)
    o_ref[...] = h_ref[...] + _dot(y_c, wo_ref[_Y_C:_Y_C + SC_WIDTH, :])

    @pl.when(s == n_s - 1)
    def _():
        nsc_ref[...] = ev_ref[tm + _HALO_SHORT - (SC_K - 1):tm + _HALO_SHORT, :]

    ev_ref[0:_HALO_SHORT, :] = ev_ref[tm:tm + _HALO_SHORT, :]

    eu_ref[_HALO_POOL:_HALO_POOL + tm, :] = z_ref[:, _O_U:_O_U + POOL_WIDTH]
    pos = s * tm + lax.broadcasted_iota(jnp.int32, (tm, 1), 0)
    for g, w in enumerate(POOL_WINDOWS):
        c0 = g * POOL_GW
        u = eu_ref[_HALO_POOL:_HALO_POOL + tm, c0:c0 + POOL_GW]
        wsum = u
        for k in range(1, w):
            wsum = wsum + eu_ref[_HALO_POOL - k:_HALO_POOL - k + tm, c0:c0 + POOL_GW]
        cnt = jnp.minimum(w, pos + 1).astype(_F32)
        d = wsum / cnt - u
        ya_ref[:, c0:c0 + POOL_GW] = (
            _dot(d, wpool_ref[g]) * pscale_ref[:, c0:c0 + POOL_GW]).astype(_BF16)
    o_ref[...] += _dot(ya_ref[...], wo_ref[_Y_A:_Y_A + POOL_WIDTH, :])
    eu_ref[0:_HALO_POOL, :] = eu_ref[tm:tm + _HALO_POOL, :]

    eg_ref[_HALO_CONF:_HALO_CONF + tm, :] = (
        z_ref[:, _O_GA:_O_GA + CONV_WIDTH] * jax.nn.sigmoid(z_ref[:, _O_GB:_O_GB + CONV_WIDTH]))
    for t0 in range(0, tm, sub):
        for c0 in range(0, CONV_WIDTH, _LANE):
            cb_ref[t0:t0 + sub, c0:c0 + _LANE] = (
                _conv31_block(eg_ref, dw_ref, c0, t0, sub) + dwb_ref[:, c0:c0 + _LANE])
        act = _layernorm_silu(cb_ref[t0:t0 + sub, :], lng_ref, lnb_ref).astype(_BF16)
        y_b = (_dot(act, pwbf_ref[...]) + pwb_ref[...]).astype(_BF16)
        o_ref[t0:t0 + sub, :] += _dot(y_b, wo_ref[_Y_B:_Y_B + CONV_WIDTH, :])

    @pl.when(s == n_s - 1)
    def _():
        nconf_ref[...] = eg_ref[tm + _HALO_CONF - (CONF_K - 1):tm + _HALO_CONF, :]

    eg_ref[0:_HALO_CONF, :] = eg_ref[tm:tm + _HALO_CONF, :]


def _mix_out_prompt(z, h, nb, layer, mix_w, w_out, *, tm, sub):
    m = z.shape[0]
    n_s = m // nb // tm
    row = lambda b, s: (b * n_s + s, 0)
    return pl.pallas_call(
        functools.partial(_mix_out_prompt_kernel, tm=tm, sub=sub, n_s=n_s),
        out_shape=(
            jax.ShapeDtypeStruct((m, D_MODEL), _F32),
            jax.ShapeDtypeStruct((nb, CONF_K - 1, CONV_WIDTH), _F32),
            jax.ShapeDtypeStruct((nb, SC_K - 1, SC_WIDTH), _F32),
        ),
        grid=(nb, n_s),
        in_specs=([pl.BlockSpec((tm, IN_COLS), row), pl.BlockSpec((tm, D_MODEL), row)]
                  + _mixer_weight_specs(layer)
                  + [_single((None, D_MODEL, D_MODEL), lambda b, s: (layer, 0, 0))]),
        out_specs=(
            pl.BlockSpec((tm, D_MODEL), row),
            pl.BlockSpec((None, CONF_K - 1, CONV_WIDTH), lambda b, s: (b, 0, 0)),
            pl.BlockSpec((None, SC_K - 1, SC_WIDTH), lambda b, s: (b, 0, 0)),
        ),
        scratch_shapes=[
            pltpu.VMEM((_HALO_POOL + tm, POOL_WIDTH), _F32),
            pltpu.VMEM((_HALO_CONF + tm + _SUBLANE, CONV_WIDTH), _F32),
            pltpu.VMEM((_HALO_SHORT + tm, SC_WIDTH), _F32),
            pltpu.VMEM((tm, CONV_WIDTH), _F32),
            pltpu.VMEM((tm, POOL_WIDTH), _BF16),
            pltpu.VMEM((D_MODEL, D_MODEL), _BF16),
            pltpu.VMEM((CONV_WIDTH, CONV_WIDTH), _BF16),
        ],
        compiler_params=_params("arbitrary", "arbitrary"),
        name="mix_out_prompt",
    )(z, h, *_mixer_weight_args(*mix_w), w_out)


def _mix_out_sample_kernel(z_ref, h_ref, spool_ref, sconf_ref, ssc_ref,
                           wpool_ref, pscale_ref, dw_ref, dwb_ref, lng_ref, lnb_ref,
                           pw_ref, pwb_ref, scw_ref, wout_ref,
                           o_ref, g_ref, v_ref):
    out = h_ref[...]
    for g, w in enumerate(POOL_WINDOWS):
        c0 = g * POOL_GW
        u = z_ref[:, _O_U + c0:_O_U + c0 + POOL_GW]
        wsum = u + jnp.sum(spool_ref[:, POOL_BUF - (w - 1):POOL_BUF, c0:c0 + POOL_GW], axis=1)
        d = wsum / jnp.float32(min(w, PAST_LEN + 1)) - u
        y_a = _dot(d, wpool_ref[g]) * pscale_ref[:, c0:c0 + POOL_GW]
        out = out + _dot(y_a, wout_ref[_Y_A + c0:_Y_A + c0 + POOL_GW, :])

    gl = z_ref[:, _O_GA:_O_GA + CONV_WIDTH] * jax.nn.sigmoid(z_ref[:, _O_GB:_O_GB + CONV_WIDTH])
    g_ref[...] = gl
    acc = dw_ref[CONF_K - 1:CONF_K, :] * gl
    acc = acc + jnp.sum(sconf_ref[...] * dw_ref[0:CONF_K - 1, :][None], axis=1)
    y_b = _dot(_layernorm_silu(acc + dwb_ref[...], lng_ref, lnb_ref), pw_ref[...]) + pwb_ref[...]
    out = out + _dot(y_b, wout_ref[_Y_B:_Y_B + CONV_WIDTH, :])

    v = z_ref[:, _O_SC:_O_SC + SC_WIDTH] * z_ref[:, _O_SH:_O_SH + SC_WIDTH]
    v_ref[...] = v
    cv = jnp.sum(ssc_ref[...] * scw_ref[0:SC_K - 1, :][None], axis=1) + scw_ref[SC_K - 1:SC_K, :] * v
    y_c = z_ref[:, _O_SB:_O_SB + SC_WIDTH] * cv
    o_ref[...] = out + _dot(y_c, wout_ref[_Y_C:_Y_C + SC_WIDTH, :])


def _mix_out_sample(z, h, state_pool, state_conf, state_sc, layer, mix_w, w_out):
    m = z.shape[0]
    full = lambda a: pl.BlockSpec(a.shape, lambda i: (0,) * a.ndim)
    state = lambda a: _single((None,) + a.shape[1:], lambda i: (layer, 0, 0, 0))
    return pl.pallas_call(
        _mix_out_sample_kernel,
        out_shape=(
            jax.ShapeDtypeStruct((m, D_MODEL), _F32),
            jax.ShapeDtypeStruct((m, CONV_WIDTH), _F32),
            jax.ShapeDtypeStruct((m, SC_WIDTH), _F32),
        ),
        grid=(1,),
        in_specs=([full(z), full(h), state(state_pool), state(state_conf), state(state_sc)]
                  + _mixer_weight_specs(layer)
                  + [_single((None, D_MODEL, D_MODEL), lambda i: (layer, 0, 0))]),
        out_specs=(
            pl.BlockSpec((m, D_MODEL), lambda i: (0, 0)),
            pl.BlockSpec((m, CONV_WIDTH), lambda i: (0, 0)),
            pl.BlockSpec((m, SC_WIDTH), lambda i: (0, 0)),
        ),
        compiler_params=_params("arbitrary"),
        name="mix_out_sample",
    )(z, h, state_pool, state_conf, state_sc, *_mixer_weight_args(*mix_w), w_out)


def _ffn_prompt_kernel(h_ref, g_ref, wa_ref, wb_ref, ka_ref, kb_ref, wd_ref,
                       o_ref, nfa_ref, nfb_ref,
                       hn_ref, ea_ref, eb_ref, ca_ref, cb_ref, *, tm, sub, n_s):
    c = pl.program_id(1)
    s = pl.program_id(0) % n_s

    @pl.when(c == 0)
    def _():
        h = h_ref[...]
        hn_ref[...] = _rms(h, g_ref[...]).astype(_BF16)
        o_ref[...] = h

    @pl.when(s == 0)
    def _():
        ca_ref[c] = jnp.zeros(ca_ref.shape[1:], _F32)
        cb_ref[c] = jnp.zeros(cb_ref.shape[1:], _F32)

    wa = wa_ref[...].astype(_BF16)
    wb = wb_ref[...].astype(_BF16)
    wd = wd_ref[...].astype(_BF16)
    ea_ref[0:_HALO_SHORT, :] = ca_ref[c]
    eb_ref[0:_HALO_SHORT, :] = cb_ref[c]

    for r0 in range(0, tm, sub):
        hn = hn_ref[r0:r0 + sub, :]
        ea_ref[_HALO_SHORT + r0:_HALO_SHORT + r0 + sub, :] = _dot(hn, wa)
        eb_ref[_HALO_SHORT + r0:_HALO_SHORT + r0 + sub, :] = _dot(hn, wb)

    def conv(e_ref, k_ref, r0):
        lo = _HALO_SHORT + r0
        out = k_ref[0:1, :] * e_ref[lo - 2:lo - 2 + sub, :]
        out = out + k_ref[1:2, :] * e_ref[lo - 1:lo - 1 + sub, :]
        return out + k_ref[2:3, :] * e_ref[lo:lo + sub, :]

    for r0 in range(0, tm, sub):
        act = (_silu(conv(ea_ref, ka_ref, r0)) * conv(eb_ref, kb_ref, r0)).astype(_BF16)
        o_ref[r0:r0 + sub, :] += _dot(act, wd)

    ca_ref[c] = ea_ref[tm:tm + _HALO_SHORT, :]
    cb_ref[c] = eb_ref[tm:tm + _HALO_SHORT, :]
    nfa_ref[...] = ea_ref[tm + _HALO_SHORT - (FFN_K - 1):tm + _HALO_SHORT, :]
    nfb_ref[...] = eb_ref[tm + _HALO_SHORT - (FFN_K - 1):tm + _HALO_SHORT, :]


def _ffn_prompt(h, nb, norm_g, w_up, ffn_conv, w_down, layer, *, tm, tf, sub):
    m = h.shape[0]
    n_s = m // nb // tm
    n_c = D_FF // tf
    return pl.pallas_call(
        functools.partial(_ffn_prompt_kernel, tm=tm, sub=sub, n_s=n_s),
        out_shape=(
            jax.ShapeDtypeStruct((m, D_MODEL), _F32),
            jax.ShapeDtypeStruct((m // tm, FFN_K - 1, D_FF), _F32),
            jax.ShapeDtypeStruct((m // tm, FFN_K - 1, D_FF), _F32),
        ),
        grid=(m // tm, n_c),
        in_specs=[
            _single((tm, D_MODEL), lambda i, c: (i, 0)),
            pl.BlockSpec((None, 1, D_MODEL), lambda i, c: (layer, 0, 0)),
            pl.BlockSpec((None, D_MODEL, tf), lambda i, c: (layer, 0, c)),
            pl.BlockSpec((None, D_MODEL, tf), lambda i, c: (layer, 0, n_c + c)),
            pl.BlockSpec((None, FFN_K, tf), lambda i, c: (layer, 0, c)),
            pl.BlockSpec((None, FFN_K, tf), lambda i, c: (layer, 0, n_c + c)),
            pl.BlockSpec((None, tf, D_MODEL), lambda i, c: (layer, c, 0)),
        ],
        out_specs=(
            pl.BlockSpec((tm, D_MODEL), lambda i, c: (i, 0)),
            pl.BlockSpec((None, FFN_K - 1, tf), lambda i, c: (i, 0, c)),
            pl.BlockSpec((None, FFN_K - 1, tf), lambda i, c: (i, 0, c)),
        ),
        scratch_shapes=[
            pltpu.VMEM((tm, D_MODEL), _BF16),
            pltpu.VMEM((_HALO_SHORT + tm, tf), _F32),
            pltpu.VMEM((_HALO_SHORT + tm, tf), _F32),
            pltpu.VMEM((n_c, _HALO_SHORT, tf), _F32),
            pltpu.VMEM((n_c, _HALO_SHORT, tf), _F32),
        ],
        compiler_params=_params("arbitrary", "arbitrary"),
        name="ffn_prompt",
    )(h, _vec3(norm_g), w_up, w_up, ffn_conv, ffn_conv, w_down)


def _ffn_sample_kernel(h_ref, g_ref, wa_ref, wb_ref, ka_ref, kb_ref, wd_ref, sa_ref, sb_ref,
                       o_ref, ua_ref, ub_ref):
    c = pl.program_id(0)

    @pl.when(c == 0)
    def _():
        o_ref[...] = h_ref[...]

    hn = _rms(h_ref[...], g_ref[...])
    ua = _dot(hn, wa_ref[...])
    ub = _dot(hn, wb_ref[...])
    ua_ref[...] = ua
    ub_ref[...] = ub
    a = jnp.sum(sa_ref[...] * ka_ref[0:FFN_K - 1, :][None], axis=1) + ka_ref[FFN_K - 1:FFN_K, :] * ua
    b = jnp.sum(sb_ref[...] * kb_ref[0:FFN_K - 1, :][None], axis=1) + kb_ref[FFN_K - 1:FFN_K, :] * ub
    o_ref[...] += _dot(_silu(a) * b, wd_ref[...])


def _ffn_sample(h, norm_g, w_up, ffn_conv, w_down, layer, state_ffn, *, tf):
    m = h.shape[0]
    n_c = D_FF // tf
    lo = lambda c: (0, c)
    hi = lambda c: (0, n_c + c)
    return pl.pallas_call(
        _ffn_sample_kernel,
        out_shape=(
            jax.ShapeDtypeStruct((m, D_MODEL), _F32),
            jax.ShapeDtypeStruct((m, D_FF), _F32),
            jax.ShapeDtypeStruct((m, D_FF), _F32),
        ),
        grid=(n_c,),
        in_specs=[
            pl.BlockSpec((m, D_MODEL), lambda c: (0, 0)),
            pl.BlockSpec((None, 1, D_MODEL), lambda c: (layer, 0, 0)),
            pl.BlockSpec((None, D_MODEL, tf), lambda c: (layer, 0, c)),
            pl.BlockSpec((None, D_MODEL, tf), lambda c: (layer, 0, n_c + c)),
            pl.BlockSpec((None, FFN_K, tf), lambda c: (layer, 0, c)),
            pl.BlockSpec((None, FFN_K, tf), lambda c: (layer, 0, n_c + c)),
            pl.BlockSpec((None, tf, D_MODEL), lambda c: (layer, c, 0)),
            pl.BlockSpec((None, m, FFN_K - 1, tf), lambda c: (layer, 0, 0, c)),
            pl.BlockSpec((None, m, FFN_K - 1, tf), lambda c: (layer, 0, 0, n_c + c)),
        ],
        out_specs=(
            pl.BlockSpec((m, D_MODEL), lambda c: (0, 0)),
            pl.BlockSpec((m, tf), lo),
            pl.BlockSpec((m, tf), lo),
        ),
        compiler_params=_params("arbitrary"),
        name="ffn_sample",
    )(h, _vec3(norm_g), w_up, w_up, ffn_conv, ffn_conv, w_down, state_ffn, state_ffn)


def _ple_kernel(h_ref, g_ref, gate_ref, p_ref, proj_ref, gf_ref, o_ref, gate_bf, proj_bf,
                *, final, tm, sub):
    @pl.when(pl.program_id(0) == 0)
    def _():
        gate_bf[...] = gate_ref[...].astype(_BF16)
        proj_bf[...] = proj_ref[...].astype(_BF16)

    for r0 in range(0, tm, sub):
        h = h_ref[r0:r0 + sub, :]
        hn = _rms(h, g_ref[...]).astype(_BF16)
        gate = jax.nn.sigmoid(_dot(hn, gate_bf[...]))
        out = h + _dot(p_ref[r0:r0 + sub, :].astype(_BF16), proj_bf[...]) * gate
        if final:
            out = _rms(out, gf_ref[...])
        o_ref[r0:r0 + sub, :] = out


def _ple(h, norm_g, gate_w, p, proj_w, layer, final_g, *, tm, sub, final, name):
    m = h.shape[0]
    return pl.pallas_call(
        functools.partial(_ple_kernel, final=final, tm=tm, sub=sub),
        out_shape=jax.ShapeDtypeStruct((m, D_MODEL), _F32),
        grid=(m // tm,),
        in_specs=[
            pl.BlockSpec((tm, D_MODEL), lambda i: (i, 0)),
            pl.BlockSpec((None, 1, D_MODEL), lambda i: (layer, 0, 0)),
            _single((None, D_MODEL, D_MODEL), lambda i: (layer, 0, 0)),
            pl.BlockSpec((None, tm, PLE_DIM), lambda i: (layer, i, 0)),
            _single((None, PLE_DIM, D_MODEL), lambda i: (layer, 0, 0)),
            pl.BlockSpec((1, D_MODEL), lambda i: (0, 0)),
        ],
        out_specs=pl.BlockSpec((tm, D_MODEL), lambda i: (i, 0)),
        scratch_shapes=[
            pltpu.VMEM((D_MODEL, D_MODEL), _BF16),
            pltpu.VMEM((PLE_DIM, D_MODEL), _BF16),
        ],
        compiler_params=_params("arbitrary"),
        name=name,
    )(h, _vec3(norm_g), gate_w, p, proj_w, final_g.reshape(1, -1))


def kernel(x_prompt, x_sample, p_prompt, p_sample, state_pool, state_conf, state_sc, state_ffn,
           norm_mix, w_in, w_pool, pool_scale, conf_dw, conf_dw_b, conf_ln_g, conf_ln_b,
           conf_pw, conf_pw_b, sc_conv, w_out, norm_ffn, w_up, ffn_conv, w_down,
           norm_ple, ple_gate, ple_proj, norm_final):
    nb, seq, _ = x_prompt.shape
    ns = x_sample.shape[0]
    depth = w_in.shape[0]
    hp = x_prompt.reshape(nb * seq, D_MODEL)
    hs = x_sample.reshape(ns, D_MODEL)
    pp = p_prompt.reshape(depth, nb * seq, PLE_DIM)
    ps = p_sample.reshape(depth, ns, PLE_DIM)
    mix_w = (w_pool, pool_scale, conf_dw, conf_dw_b, conf_ln_g, conf_ln_b, conf_pw, conf_pw_b, sc_conv)

    prompt_states = [[] for _ in range(4)]
    sample_rows = [[] for _ in range(4)]
    for i in range(depth):
        last = i == depth - 1

        z = _norm_matmul(hp, norm_mix, w_in, i, tm=_TM_IN, tn=_TN_IN, sub=_SUB_IN,
                         name="in_proj_prompt")
        npool_p = z.reshape(nb, seq, IN_COLS)[:, seq - POOL_BUF:, :POOL_WIDTH]
        h1, nconf_p, nsc_p = _mix_out_prompt(z, hp, nb, i, mix_w, w_out, tm=_TM_MIX, sub=_SUB_MIX)
        h2, nfa, nfb = _ffn_prompt(h1, nb, norm_ffn, w_up, ffn_conv, w_down, i,
                                   tm=_TM_FFN, tf=_TF_FFN, sub=_SUB_FFN)
        tiles_per_seq = nfa.shape[0] // nb
        nffn_p = jnp.concatenate([nfa, nfb], axis=-1)[tiles_per_seq - 1::tiles_per_seq]
        hp = _ple(h2, norm_ple, ple_gate, pp, ple_proj, i, norm_final,
                  tm=_TM_PLE, sub=_SUB_PLE, final=last, name="ple_prompt")

        zs = _norm_matmul(hs, norm_mix, w_in, i, tm=ns, tn=_TN_IN, sub=ns, name="in_proj_sample")
        h1s, g_s, v_s = _mix_out_sample(zs, hs, state_pool, state_conf, state_sc, i, mix_w, w_out)
        h2s, ua_s, ub_s = _ffn_sample(h1s, norm_ffn, w_up, ffn_conv, w_down, i, state_ffn,
                                      tf=_TF_FFN_SAMPLE)
        hs = _ple(h2s, norm_ple, ple_gate, ps, ple_proj, i, norm_final,
                  tm=ns, sub=ns, final=last, name="ple_sample")

        for lst, val in zip(prompt_states, (npool_p, nconf_p, nsc_p, nffn_p)):
            lst.append(val)
        for lst, val in zip(sample_rows, (zs[:, :POOL_WIDTH], g_s, v_s,
                                          jnp.concatenate([ua_s, ub_s], axis=-1))):
            lst.append(val)

    def shifted(state, rows):
        return jnp.concatenate([state[:, :, 1:], jnp.stack(rows)[:, :, None]], axis=2)

    new_p = [jnp.stack(l) for l in prompt_states]
    new_s = [shifted(st, rows) for st, rows in
             zip((state_pool, state_conf, state_sc, state_ffn), sample_rows)]
    y_prompt = hp.reshape(nb, seq, D_MODEL)
    y_sample = hs.reshape(ns, 1, D_MODEL)
    return (y_prompt, y_sample, new_p[0], new_s[0], new_p[1], new_s[1],
            new_p[2], new_s[2], new_p[3], new_s[3])
```

```python
import functools

import jax
import jax.numpy as jnp
from jax import lax
from jax.experimental import pallas as pl
from jax.experimental.pallas import tpu as pltpu

D_MODEL = 2048
POOL_WIDTH = 512
POOL_WINDOWS = (2, 4, 8, 16)
POOL_GW = 128
POOL_BUF = 15
CONV_WIDTH = 768
CONF_K = 31
SC_WIDTH = 768
SC_K = 3
FFN_K = 3
D_FF = 5632
PLE_DIM = 256
IN_COLS = POOL_WIDTH + 2 * CONV_WIDTH + 3 * SC_WIDTH
EPS = 1e-6
PAST_LEN = 16384

_O_U = 0
_O_GA = _O_U + POOL_WIDTH
_O_GB = _O_GA + CONV_WIDTH
_O_SB = _O_GB + CONV_WIDTH
_O_SC = _O_SB + SC_WIDTH
_O_SH = _O_SC + SC_WIDTH
_Y_A = 0
_Y_B = POOL_WIDTH
_Y_C = POOL_WIDTH + CONV_WIDTH

_VMEM_LIMIT = 56 * 1024 * 1024
_LANE = 128
_SUBLANE = 8
_BF16_ROWS = 16

_HALO_POOL = 16
_HALO_CONF = 32
_HALO_SHORT = 8

_TM_IN, _TN_IN, _SUB_IN = 512, IN_COLS // 2, 256
_TM_MIX = 256
_CONV_ROWS = 128
_CONV_BLOCKS_PER_TICK = 3
_OUT_COLS_PER_TICK = 256
_TM_FFN, _TF_FFN, _SUB_FFN = 1024, 512, 512
_TM_PLE, _SUB_PLE = 512, 256

_BF16 = jnp.bfloat16
_F32 = jnp.float32


def _params(*sem):
    return pltpu.CompilerParams(dimension_semantics=sem, vmem_limit_bytes=_VMEM_LIMIT)


def _rms(x, g):
    ms = jnp.mean(x * x, axis=-1, keepdims=True)
    return x * lax.rsqrt(ms + EPS) * g


def _dot(a, b):
    return jnp.dot(a, b, preferred_element_type=_F32)


def _silu(x):
    return x * jax.nn.sigmoid(x)


def _single(block, index_map):
    return pl.BlockSpec(block, index_map, pipeline_mode=pl.Buffered(1))


def _vec3(a):
    return a.reshape(a.shape[0], 1, a.shape[1])


def _norm_matmul_kernel(x_ref, g_ref, w_ref, o_ref, wb_ref, *, tm, sub):
    @pl.when(pl.program_id(1) == 0)
    def _():
        wb_ref[...] = w_ref[...].astype(_BF16)

    for r0 in range(0, tm, sub):
        hn = _rms(x_ref[r0:r0 + sub, :], g_ref[...]).astype(_BF16)
        o_ref[r0:r0 + sub, :] = _dot(hn, wb_ref[...])


def _norm_matmul(x, g, w, layer, *, tm, tn, sub, name):
    m, k = x.shape
    n = w.shape[2]
    return pl.pallas_call(
        functools.partial(_norm_matmul_kernel, tm=tm, sub=sub),
        out_shape=jax.ShapeDtypeStruct((m, n), _F32),
        grid=(n // tn, m // tm),
        in_specs=[
            pl.BlockSpec((tm, k), lambda j, i: (i, 0)),
            pl.BlockSpec((None, 1, k), lambda j, i: (layer, 0, 0)),
            _single((None, k, tn), lambda j, i: (layer, 0, j)),
        ],
        out_specs=pl.BlockSpec((tm, tn), lambda j, i: (i, j)),
        scratch_shapes=[pltpu.VMEM((k, tn), _BF16)],
        compiler_params=_params("arbitrary", "arbitrary"),
        name=name,
    )(x, _vec3(g), w)


def _mixer_weight_specs(layer):
    l3 = lambda *_: (layer, 0, 0)
    l4 = lambda *_: (layer, 0, 0, 0)
    return [
        _single((None, len(POOL_WINDOWS), POOL_GW, POOL_GW), l4),
        pl.BlockSpec((None, 1, POOL_WIDTH), l3),
        pl.BlockSpec((None, CONF_K, CONV_WIDTH), l3),
        pl.BlockSpec((None, 1, CONV_WIDTH), l3),
        pl.BlockSpec((None, 1, CONV_WIDTH), l3),
        pl.BlockSpec((None, 1, CONV_WIDTH), l3),
        _single((None, CONV_WIDTH, CONV_WIDTH), l3),
        pl.BlockSpec((None, 1, CONV_WIDTH), l3),
        pl.BlockSpec((None, SC_K, SC_WIDTH), l3),
    ]


def _mixer_weight_args(w_pool, pool_scale, conf_dw, conf_dw_b, conf_ln_g, conf_ln_b,
                       conf_pw, conf_pw_b, sc_conv):
    return (w_pool, _vec3(pool_scale), conf_dw, _vec3(conf_dw_b), _vec3(conf_ln_g),
            _vec3(conf_ln_b), conf_pw, _vec3(conf_pw_b), sc_conv)


def _layernorm_silu(cb, lng_ref, lnb_ref):
    mu = jnp.mean(cb, axis=-1, keepdims=True)
    xc = cb - mu
    var = jnp.mean(xc * xc, axis=-1, keepdims=True)
    return _silu(xc * lax.rsqrt(var + EPS) * lng_ref[...] + lnb_ref[...])


def _conv31_block(eg_ref, dw_ref, c0, t0, rows):
    base = _HALO_CONF - (CONF_K - 1)
    acc = None
    for r in range(_SUBLANE):
        p = None
        for q in range((CONF_K + base) // _SUBLANE + 1):
            k = _SUBLANE * q + r - base
            if 0 <= k < CONF_K:
                lo = t0 + _SUBLANE * q
                term = dw_ref[k:k + 1, c0:c0 + _LANE] * eg_ref[lo:lo + rows + _SUBLANE, c0:c0 + _LANE]
                p = term if p is None else p + term
        shifted = p[r:r + rows, :]
        acc = shifted if acc is None else acc + shifted
    return acc


def _mixers_prompt(z_ref, wpool_ref, pscale_ref, dw_ref, dwb_ref, lng_ref, lnb_ref,
                   pwbf_ref, pwb_ref, scw_ref, y_ref, nconf_ref, nsc_ref,
                   eu_ref, eg_ref, ev_ref, cb_ref, *, tm, s, tick):
    tick()
    ev_ref[_HALO_SHORT:_HALO_SHORT + tm, :] = (
        z_ref[:, _O_SC:_O_SC + SC_WIDTH] * z_ref[:, _O_SH:_O_SH + SC_WIDTH])
    cv = scw_ref[0:1, :] * ev_ref[_HALO_SHORT - 2:_HALO_SHORT - 2 + tm, :]
    cv = cv + scw_ref[1:2, :] * ev_ref[_HALO_SHORT - 1:_HALO_SHORT - 1 + tm, :]
    cv = cv + scw_ref[2:3, :] * ev_ref[_HALO_SHORT:_HALO_SHORT + tm, :]
    y_ref[:, _Y_C:_Y_C + SC_WIDTH] = (z_ref[:, _O_SB:_O_SB + SC_WIDTH] * cv).astype(_BF16)
    nsc_ref[...] = ev_ref[tm + _HALO_SHORT - (SC_K - 1):tm + _HALO_SHORT, :]
    ev_ref[0:_HALO_SHORT, :] = ev_ref[tm:tm + _HALO_SHORT, :]

    eu_ref[_HALO_POOL:_HALO_POOL + tm, :] = z_ref[:, _O_U:_O_U + POOL_WIDTH]
    pos = s * tm + lax.broadcasted_iota(jnp.int32, (tm, 1), 0)
    for g, w in enumerate(POOL_WINDOWS):
        if g % 2 == 0:
            tick()
        c0 = g * POOL_GW
        u = eu_ref[_HALO_POOL:_HALO_POOL + tm, c0:c0 + POOL_GW]
        wsum = u
        for k in range(1, w):
            wsum = wsum + eu_ref[_HALO_POOL - k:_HALO_POOL - k + tm, c0:c0 + POOL_GW]
        cnt = jnp.minimum(w, pos + 1).astype(_F32)
        d = wsum / cnt - u
        y_ref[:, _Y_A + c0:_Y_A + c0 + POOL_GW] = (
            _dot(d, wpool_ref[g]) * pscale_ref[:, c0:c0 + POOL_GW]).astype(_BF16)
    eu_ref[0:_HALO_POOL, :] = eu_ref[tm:tm + _HALO_POOL, :]

    eg_ref[_HALO_CONF:_HALO_CONF + tm, :] = (
        z_ref[:, _O_GA:_O_GA + CONV_WIDTH] * jax.nn.sigmoid(z_ref[:, _O_GB:_O_GB + CONV_WIDTH]))
    blocks = [(t0, c0) for t0 in range(0, tm, _CONV_ROWS) for c0 in range(0, CONV_WIDTH, _LANE)]
    for n, (t0, c0) in enumerate(blocks):
        if n % _CONV_BLOCKS_PER_TICK == 0:
            tick()
        cb_ref[t0:t0 + _CONV_ROWS, c0:c0 + _LANE] = (
            _conv31_block(eg_ref, dw_ref, c0, t0, _CONV_ROWS) + dwb_ref[:, c0:c0 + _LANE])
    tick()
    act = _layernorm_silu(cb_ref[...], lng_ref, lnb_ref).astype(_BF16)
    y_ref[:, _Y_B:_Y_B + CONV_WIDTH] = (_dot(act, pwbf_ref[...]) + pwb_ref[...]).astype(_BF16)
    nconf_ref[...] = eg_ref[tm + _HALO_CONF - (CONF_K - 1):tm + _HALO_CONF, :]
    eg_ref[0:_HALO_CONF, :] = eg_ref[tm:tm + _HALO_CONF, :]


def _mix_out_prompt_kernel(z_ref, h_ref, wpool_ref, pscale_ref, dw_ref, dwb_ref, lng_ref, lnb_ref,
                           pw_ref, pwb_ref, scw_ref, wout_ref,
                           o_ref, nconf_ref, nsc_ref,
                           eu_ref, eg_ref, ev_ref, cb_ref, y0_ref, y1_ref, wo_ref, pwbf_ref,
                           *, tm, n_s, n_tiles):
    i = pl.program_id(0)
    s = jnp.minimum(i, n_tiles - 1) % n_s

    @pl.when(i == 0)
    def _():
        wo_ref[...] = wout_ref[...].astype(_BF16)
        pwbf_ref[...] = pw_ref[...].astype(_BF16)
        eg_ref[_HALO_CONF + tm:_HALO_CONF + tm + _SUBLANE, :] = jnp.zeros((_SUBLANE, CONV_WIDTH), _F32)
        y1_ref[...] = jnp.zeros(y1_ref.shape, _BF16)

    @pl.when(s == 0)
    def _():
        eu_ref[0:_HALO_POOL, :] = jnp.zeros((_HALO_POOL, POOL_WIDTH), _F32)
        eg_ref[0:_HALO_CONF, :] = jnp.zeros((_HALO_CONF, CONV_WIDTH), _F32)
        ev_ref[0:_HALO_SHORT, :] = jnp.zeros((_HALO_SHORT, SC_WIDTH), _F32)

    def step(y_prev_ref, y_cur_ref):
        pending = list(range(0, D_MODEL, _OUT_COLS_PER_TICK))

        def tick():
            if pending:
                c0 = pending.pop(0)
                cols = slice(c0, c0 + _OUT_COLS_PER_TICK)
                o_ref[:, cols] = h_ref[:, cols] + _dot(y_prev_ref[...], wo_ref[:, cols])

        _mixers_prompt(z_ref, wpool_ref, pscale_ref, dw_ref, dwb_ref, lng_ref, lnb_ref,
                       pwbf_ref, pwb_ref, scw_ref, y_cur_ref, nconf_ref, nsc_ref,
                       eu_ref, eg_ref, ev_ref, cb_ref, tm=tm, s=s, tick=tick)
        while pending:
            tick()

    @pl.when(i % 2 == 0)
    def _():
        step(y1_ref, y0_ref)

    @pl.when(i % 2 == 1)
    def _():
        step(y0_ref, y1_ref)


def _mix_out_prompt(z, h, nb, layer, mix_w, w_out, *, tm):
    m = z.shape[0]
    n_tiles = m // tm
    n_s = n_tiles // nb
    cur = lambda i: (jnp.minimum(i, n_tiles - 1), 0)
    prev = lambda i: (jnp.maximum(i - 1, 0), 0)
    seq = lambda i: (jnp.minimum(i, n_tiles - 1) // n_s, 0, 0)
    return pl.pallas_call(
        functools.partial(_mix_out_prompt_kernel, tm=tm, n_s=n_s, n_tiles=n_tiles),
        out_shape=(
            jax.ShapeDtypeStruct((m, D_MODEL), _F32),
            jax.ShapeDtypeStruct((nb, CONF_K - 1, CONV_WIDTH), _F32),
            jax.ShapeDtypeStruct((nb, SC_K - 1, SC_WIDTH), _F32),
        ),
        grid=(n_tiles + 1,),
        in_specs=([pl.BlockSpec((tm, IN_COLS), cur), pl.BlockSpec((tm, D_MODEL), prev)]
                  + _mixer_weight_specs(layer)
                  + [_single((None, D_MODEL, D_MODEL), lambda i: (layer, 0, 0))]),
        out_specs=(
            pl.BlockSpec((tm, D_MODEL), prev),
            pl.BlockSpec((None, CONF_K - 1, CONV_WIDTH), seq),
            pl.BlockSpec((None, SC_K - 1, SC_WIDTH), seq),
        ),
        scratch_shapes=[
            pltpu.VMEM((_HALO_POOL + tm, POOL_WIDTH), _F32),
            pltpu.VMEM((_HALO_CONF + tm + _SUBLANE, CONV_WIDTH), _F32),
            pltpu.VMEM((_HALO_SHORT + tm, SC_WIDTH), _F32),
            pltpu.VMEM((tm, CONV_WIDTH), _F32),
            pltpu.VMEM((tm, D_MODEL), _BF16),
            pltpu.VMEM((tm, D_MODEL), _BF16),
            pltpu.VMEM((D_MODEL, D_MODEL), _BF16),
            pltpu.VMEM((CONV_WIDTH, CONV_WIDTH), _BF16),
        ],
        compiler_params=_params("arbitrary"),
        name="mix_out_prompt",
    )(z, h, *_mixer_weight_args(*mix_w), w_out)


def _mix_out_sample_kernel(z_ref, h_ref, spool_ref, sconf_ref, ssc_ref,
                           wpool_ref, pscale_ref, dw_ref, dwb_ref, lng_ref, lnb_ref,
                           pw_ref, pwb_ref, scw_ref, wout_ref,
                           o_ref, g_ref, v_ref):
    out = h_ref[...]
    for g, w in enumerate(POOL_WINDOWS):
        c0 = g * POOL_GW
        u = z_ref[:, _O_U + c0:_O_U + c0 + POOL_GW]
        wsum = u + jnp.sum(spool_ref[:, POOL_BUF - (w - 1):POOL_BUF, c0:c0 + POOL_GW], axis=1)
        d = wsum / jnp.float32(min(w, PAST_LEN + 1)) - u
        y_a = _dot(d, wpool_ref[g]) * pscale_ref[:, c0:c0 + POOL_GW]
        out = out + _dot(y_a, wout_ref[_Y_A + c0:_Y_A + c0 + POOL_GW, :])

    gl = z_ref[:, _O_GA:_O_GA + CONV_WIDTH] * jax.nn.sigmoid(z_ref[:, _O_GB:_O_GB + CONV_WIDTH])
    g_ref[...] = gl
    acc = dw_ref[CONF_K - 1:CONF_K, :] * gl
    acc = acc + jnp.sum(sconf_ref[...] * dw_ref[0:CONF_K - 1, :][None], axis=1)
    y_b = _dot(_layernorm_silu(acc + dwb_ref[...], lng_ref, lnb_ref), pw_ref[...]) + pwb_ref[...]
    out = out + _dot(y_b, wout_ref[_Y_B:_Y_B + CONV_WIDTH, :])

    v = z_ref[:, _O_SC:_O_SC + SC_WIDTH] * z_ref[:, _O_SH:_O_SH + SC_WIDTH]
    v_ref[...] = v
    cv = jnp.sum(ssc_ref[...] * scw_ref[0:SC_K - 1, :][None], axis=1) + scw_ref[SC_K - 1:SC_K, :] * v
    y_c = z_ref[:, _O_SB:_O_SB + SC_WIDTH] * cv
    o_ref[...] = out + _dot(y_c, wout_ref[_Y_C:_Y_C + SC_WIDTH, :])


def _mix_out_sample(z, h, state_pool, state_conf, state_sc, layer, mix_w, w_out):
    m = z.shape[0]
    full = lambda a: pl.BlockSpec(a.shape, lambda i: (0,) * a.ndim)
    state = lambda a: _single((None,) + a.shape[1:], lambda i: (layer, 0, 0, 0))
    return pl.pallas_call(
        _mix_out_sample_kernel,
        out_shape=(
            jax.ShapeDtypeStruct((m, D_MODEL), _F32),
            jax.ShapeDtypeStruct((m, CONV_WIDTH), _F32),
            jax.ShapeDtypeStruct((m, SC_WIDTH), _F32),
        ),
        grid=(1,),
        in_specs=([full(z), full(h), state(state_pool), state(state_conf), state(state_sc)]
                  + _mixer_weight_specs(layer)
                  + [_single((None, D_MODEL, D_MODEL), lambda i: (layer, 0, 0))]),
        out_specs=(
            pl.BlockSpec((m, D_MODEL), lambda i: (0, 0)),
            pl.BlockSpec((m, CONV_WIDTH), lambda i: (0, 0)),
            pl.BlockSpec((m, SC_WIDTH), lambda i: (0, 0)),
        ),
        compiler_params=_params("arbitrary"),
        name="mix_out_sample",
    )(z, h, state_pool, state_conf, state_sc, *_mixer_weight_args(*mix_w), w_out)


def _ffn_kernel(h_ref, hs_ref, g_ref, wa_ref, wb_ref, ka_ref, kb_ref, wd_ref, sa_ref, sb_ref,
                o_ref, os_ref, nfa_ref, nfb_ref, uas_ref, ubs_ref,
                hn_ref, ea_ref, eb_ref, ca_ref, cb_ref, *, tm, ts, sub, n_s):
    c = pl.program_id(1)
    s = pl.program_id(0) % n_s

    @pl.when(c == 0)
    def _():
        h = h_ref[...]
        hs = hs_ref[...]
        hn_ref[0:tm, :] = _rms(h, g_ref[...]).astype(_BF16)
        hn_ref[tm:tm + ts, :] = _rms(hs, g_ref[...]).astype(_BF16)
        o_ref[...] = h
        os_ref[...] = hs

    @pl.when(s == 0)
    def _():
        ca_ref[c] = jnp.zeros(ca_ref.shape[1:], _F32)
        cb_ref[c] = jnp.zeros(cb_ref.shape[1:], _F32)

    wa = wa_ref[...]
    wb = wb_ref[...]
    wd = wd_ref[...]
    ea_ref[0:_HALO_SHORT, :] = ca_ref[c]
    eb_ref[0:_HALO_SHORT, :] = cb_ref[c]

    starts = list(range(0, tm, sub))
    blocks = [(r0, r0 + sub) for r0 in starts[:-1]] + [(starts[-1], tm + ts)]

    for r0, r1 in blocks:
        hn = hn_ref[r0:r1, :]
        ea_ref[_HALO_SHORT + r0:_HALO_SHORT + r1, :] = _dot(hn, wa)
        eb_ref[_HALO_SHORT + r0:_HALO_SHORT + r1, :] = _dot(hn, wb)

    def conv(e_ref, k_ref, r0, r1):
        lo, n = _HALO_SHORT + r0, r1 - r0
        out = k_ref[0:1, :] * e_ref[lo - 2:lo - 2 + n, :]
        out = out + k_ref[1:2, :] * e_ref[lo - 1:lo - 1 + n, :]
        return out + k_ref[2:3, :] * e_ref[lo:lo + n, :]

    def conv_sample(e_ref, k_ref, st_ref):
        new = e_ref[_HALO_SHORT + tm:_HALO_SHORT + tm + ts, :]
        hist = jnp.sum(st_ref[...] * k_ref[0:FFN_K - 1, :][None], axis=1)
        return hist + k_ref[FFN_K - 1:FFN_K, :] * new

    for r0, r1 in blocks:
        r1p = min(r1, tm)
        act = _silu(conv(ea_ref, ka_ref, r0, r1p)) * conv(eb_ref, kb_ref, r0, r1p)
        if r1 > tm:
            act_s = _silu(conv_sample(ea_ref, ka_ref, sa_ref)) * conv_sample(eb_ref, kb_ref, sb_ref)
            res = _dot(jnp.concatenate([act, act_s], axis=0).astype(_BF16), wd)
            o_ref[r0:tm, :] += res[0:tm - r0, :]
            os_ref[...] += res[tm - r0:tm - r0 + ts, :]
        else:
            o_ref[r0:r1, :] += _dot(act.astype(_BF16), wd)

    ca_ref[c] = ea_ref[tm:tm + _HALO_SHORT, :]
    cb_ref[c] = eb_ref[tm:tm + _HALO_SHORT, :]
    nfa_ref[...] = ea_ref[tm + _HALO_SHORT - (FFN_K - 1):tm + _HALO_SHORT, :]
    nfb_ref[...] = eb_ref[tm + _HALO_SHORT - (FFN_K - 1):tm + _HALO_SHORT, :]
    uas_ref[...] = ea_ref[_HALO_SHORT + tm:_HALO_SHORT + tm + ts, :]
    ubs_ref[...] = eb_ref[_HALO_SHORT + tm:_HALO_SHORT + tm + ts, :]


def _ffn(h, hs, nb, norm_g, w_up, ffn_conv, w_down, state_ffn, layer, *, tm, tf, sub):
    m = h.shape[0]
    n_tiles = m // tm
    n_s = n_tiles // nb
    n_c = D_FF // tf
    ts = hs.shape[0] // n_tiles
    assert ts * n_tiles == hs.shape[0] and ts % _BF16_ROWS == 0, (hs.shape, n_tiles)
    return pl.pallas_call(
        functools.partial(_ffn_kernel, tm=tm, ts=ts, sub=sub, n_s=n_s),
        out_shape=(
            jax.ShapeDtypeStruct((m, D_MODEL), _F32),
            jax.ShapeDtypeStruct(hs.shape, _F32),
            jax.ShapeDtypeStruct((n_tiles, FFN_K - 1, D_FF), _F32),
            jax.ShapeDtypeStruct((n_tiles, FFN_K - 1, D_FF), _F32),
            jax.ShapeDtypeStruct((hs.shape[0], D_FF), _F32),
            jax.ShapeDtypeStruct((hs.shape[0], D_FF), _F32),
        ),
        grid=(n_tiles, n_c),
        in_specs=[
            _single((tm, D_MODEL), lambda i, c: (i, 0)),
            pl.BlockSpec((ts, D_MODEL), lambda i, c: (i, 0)),
            pl.BlockSpec((None, 1, D_MODEL), lambda i, c: (layer, 0, 0)),
            pl.BlockSpec((None, D_MODEL, tf), lambda i, c: (layer, 0, c)),
            pl.BlockSpec((None, D_MODEL, tf), lambda i, c: (layer, 0, n_c + c)),
            pl.BlockSpec((None, FFN_K, tf), lambda i, c: (layer, 0, c)),
            pl.BlockSpec((None, FFN_K, tf), lambda i, c: (layer, 0, n_c + c)),
            pl.BlockSpec((None, tf, D_MODEL), lambda i, c: (layer, c, 0)),
            pl.BlockSpec((None, ts, FFN_K - 1, tf), lambda i, c: (layer, i, 0, c)),
            pl.BlockSpec((None, ts, FFN_K - 1, tf), lambda i, c: (layer, i, 0, n_c + c)),
        ],
        out_specs=(
            pl.BlockSpec((tm, D_MODEL), lambda i, c: (i, 0)),
            pl.BlockSpec((ts, D_MODEL), lambda i, c: (i, 0)),
            pl.BlockSpec((None, FFN_K - 1, tf), lambda i, c: (i, 0, c)),
            pl.BlockSpec((None, FFN_K - 1, tf), lambda i, c: (i, 0, c)),
            pl.BlockSpec((ts, tf), lambda i, c: (i, c)),
            pl.BlockSpec((ts, tf), lambda i, c: (i, c)),
        ),
        scratch_shapes=[
            pltpu.VMEM((tm + ts, D_MODEL), _BF16),
            pltpu.VMEM((_HALO_SHORT + tm + ts, tf), _F32),
            pltpu.VMEM((_HALO_SHORT + tm + ts, tf), _F32),
            pltpu.VMEM((n_c, _HALO_SHORT, tf), _F32),
            pltpu.VMEM((n_c, _HALO_SHORT, tf), _F32),
        ],
        compiler_params=_params("arbitrary", "arbitrary"),
        name="ffn",
    )(h, hs, _vec3(norm_g), w_up, w_up, ffn_conv, ffn_conv, w_down, state_ffn, state_ffn)


def _ple_kernel(h_ref, g_ref, gate_ref, p_ref, proj_ref, gf_ref, o_ref, gate_bf, proj_bf,
                *, final, tm, sub):
    @pl.when(pl.program_id(0) == 0)
    def _():
        gate_bf[...] = gate_ref[...].astype(_BF16)
        proj_bf[...] = proj_ref[...].astype(_BF16)

    for r0 in range(0, tm, sub):
        h = h_ref[r0:r0 + sub, :]
        hn = _rms(h, g_ref[...]).astype(_BF16)
        gate = jax.nn.sigmoid(_dot(hn, gate_bf[...]))
        out = h + _dot(p_ref[r0:r0 + sub, :].astype(_BF16), proj_bf[...]) * gate
        if final:
            out = _rms(out, gf_ref[...])
        o_ref[r0:r0 + sub, :] = out


def _ple(h, norm_g, gate_w, p, proj_w, layer, final_g, *, tm, sub, final, name):
    m = h.shape[0]
    return pl.pallas_call(
        functools.partial(_ple_kernel, final=final, tm=tm, sub=sub),
        out_shape=jax.ShapeDtypeStruct((m, D_MODEL), _F32),
        grid=(m // tm,),
        in_specs=[
            pl.BlockSpec((tm, D_MODEL), lambda i: (i, 0)),
            pl.BlockSpec((None, 1, D_MODEL), lambda i: (layer, 0, 0)),
            _single((None, D_MODEL, D_MODEL), lambda i: (layer, 0, 0)),
            pl.BlockSpec((None, tm, PLE_DIM), lambda i: (layer, i, 0)),
            _single((None, PLE_DIM, D_MODEL), lambda i: (layer, 0, 0)),
            pl.BlockSpec((1, D_MODEL), lambda i: (0, 0)),
        ],
        out_specs=pl.BlockSpec((tm, D_MODEL), lambda i: (i, 0)),
        scratch_shapes=[
            pltpu.VMEM((D_MODEL, D_MODEL), _BF16),
            pltpu.VMEM((PLE_DIM, D_MODEL), _BF16),
        ],
        compiler_params=_params("arbitrary"),
        name=name,
    )(h, _vec3(norm_g), gate_w, p, proj_w, final_g.reshape(1, -1))


def kernel(x_prompt, x_sample, p_prompt, p_sample, state_pool, state_conf, state_sc, state_ffn,
           norm_mix, w_in, w_pool, pool_scale, conf_dw, conf_dw_b, conf_ln_g, conf_ln_b,
           conf_pw, conf_pw_b, sc_conv, w_out, norm_ffn, w_up, ffn_conv, w_down,
           norm_ple, ple_gate, ple_proj, norm_final):
    nb, seq, _ = x_prompt.shape
    ns = x_sample.shape[0]
    depth = w_in.shape[0]
    hp = x_prompt.reshape(nb * seq, D_MODEL)
    hs = x_sample.reshape(ns, D_MODEL)
    pp = p_prompt.reshape(depth, nb * seq, PLE_DIM)
    ps = p_sample.reshape(depth, ns, PLE_DIM)
    mix_w = (w_pool, pool_scale, conf_dw, conf_dw_b, conf_ln_g, conf_ln_b, conf_pw, conf_pw_b, sc_conv)

    w_up_bf = w_up.astype(_BF16)
    w_down_bf = w_down.astype(_BF16)

    prompt_states = [[] for _ in range(4)]
    sample_rows = [[] for _ in range(4)]
    for i in range(depth):
        last = i == depth - 1

        z = _norm_matmul(hp, norm_mix, w_in, i, tm=_TM_IN, tn=_TN_IN, sub=_SUB_IN,
                         name="in_proj_prompt")
        npool_p = z.reshape(nb, seq, IN_COLS)[:, seq - POOL_BUF:, :POOL_WIDTH]
        h1, nconf_p, nsc_p = _mix_out_prompt(z, hp, nb, i, mix_w, w_out, tm=_TM_MIX)
        zs = _norm_matmul(hs, norm_mix, w_in, i, tm=ns, tn=_TN_IN, sub=ns, name="in_proj_sample")
        h1s, g_s, v_s = _mix_out_sample(zs, hs, state_pool, state_conf, state_sc, i, mix_w, w_out)

        h2, h2s, nfa, nfb, ua_s, ub_s = _ffn(h1, h1s, nb, norm_ffn, w_up_bf, ffn_conv, w_down_bf,
                                             state_ffn, i, tm=_TM_FFN, tf=_TF_FFN, sub=_SUB_FFN)
        tiles_per_seq = nfa.shape[0] // nb
        nffn_p = jnp.concatenate([nfa, nfb], axis=-1)[tiles_per_seq - 1::tiles_per_seq]

        hp = _ple(h2, norm_ple, ple_gate, pp, ple_proj, i, norm_final,
                  tm=_TM_PLE, sub=_SUB_PLE, final=last, name="ple_prompt")
        hs = _ple(h2s, norm_ple, ple_gate, ps, ple_proj, i, norm_final,
                  tm=ns, sub=ns, final=last, name="ple_sample")

        for lst, val in zip(prompt_states, (npool_p, nconf_p, nsc_p, nffn_p)):
            lst.append(val)
        for lst, val in zip(sample_rows, (zs[:, :POOL_WIDTH], g_s, v_s,
                                          jnp.concatenate([ua_s, ub_s], axis=-1))):
            lst.append(val)

    def shifted(state, rows):
        return jnp.concatenate([state[:, :, 1:], jnp.stack(rows)[:, :, None]], axis=2)

    new_p = [jnp.stack(l) for l in prompt_states]
    new_s = [shifted(st, rows) for st, rows in
             zip((state_pool, state_conf, state_sc, state_ffn), sample_rows)]
    y_prompt = hp.reshape(nb, seq, D_MODEL)
    y_sample = hs.reshape(ns, 1, D_MODEL)
    return (y_prompt, y_sample, new_p[0], new_s[0], new_p[1], new_s[1],
            new_p[2], new_s[2], new_p[3], new_s[3])
```

```python
import functools

import jax
import jax.numpy as jnp
from jax import lax
from jax.experimental import pallas as pl
from jax.experimental.pallas import tpu as pltpu

D_MODEL = 2048
POOL_WIDTH = 512
POOL_WINDOWS = (2, 4, 8, 16)
POOL_GW = 128
POOL_BUF = 15
CONV_WIDTH = 768
CONF_K = 31
SC_WIDTH = 768
SC_K = 3
FFN_K = 3
D_FF = 5632
PLE_DIM = 256
IN_COLS = POOL_WIDTH + 2 * CONV_WIDTH + 3 * SC_WIDTH
EPS = 1e-6
PAST_LEN = 16384

_O_U = 0
_O_GA = _O_U + POOL_WIDTH
_O_GB = _O_GA + CONV_WIDTH
_O_SB = _O_GB + CONV_WIDTH
_O_SC = _O_SB + SC_WIDTH
_O_SH = _O_SC + SC_WIDTH
_Y_A = 0
_Y_B = POOL_WIDTH
_Y_C = POOL_WIDTH + CONV_WIDTH

_VMEM_LIMIT = 56 * 1024 * 1024
_LANE = 128
_SUBLANE = 8
_BF16_ROWS = 16

_HALO_POOL = 16
_HALO_CONF = 32
_HALO_SHORT = 8

_TM_IN, _TN_IN, _SUB_IN = 512, IN_COLS // 2, 256
_TM_MIX = 256
_CONV_ROWS = 128
_CONV_BLOCKS_PER_TICK = 3
_OUT_COLS_PER_TICK = 256
_TM_FFN, _TF_FFN, _SUB_FFN = 1024, 512, 512
_TM_PLE, _SUB_PLE = 512, 256

_BF16 = jnp.bfloat16
_F32 = jnp.float32


def _params(*sem):
    return pltpu.CompilerParams(dimension_semantics=sem, vmem_limit_bytes=_VMEM_LIMIT)


def _rms(x, g):
    ms = jnp.mean(x * x, axis=-1, keepdims=True)
    return x * lax.rsqrt(ms + EPS) * g


def _dot(a, b):
    return jnp.dot(a, b, preferred_element_type=_F32)


def _silu(x):
    return x * jax.nn.sigmoid(x)


def _single(block, index_map):
    return pl.BlockSpec(block, index_map, pipeline_mode=pl.Buffered(1))


def _vec3(a):
    return a.reshape(a.shape[0], 1, a.shape[1])


def _norm_matmul_kernel(x_ref, g_ref, w_ref, o_ref, wb_ref, *, tm, sub):
    @pl.when(pl.program_id(1) == 0)
    def _():
        wb_ref[...] = w_ref[...].astype(_BF16)

    for r0 in range(0, tm, sub):
        hn = _rms(x_ref[r0:r0 + sub, :], g_ref[...]).astype(_BF16)
        o_ref[r0:r0 + sub, :] = _dot(hn, wb_ref[...])


def _norm_matmul(x, g, w, layer, *, tm, tn, sub, name):
    m, k = x.shape
    n = w.shape[2]
    return pl.pallas_call(
        functools.partial(_norm_matmul_kernel, tm=tm, sub=sub),
        out_shape=jax.ShapeDtypeStruct((m, n), _F32),
        grid=(n // tn, m // tm),
        in_specs=[
            pl.BlockSpec((tm, k), lambda j, i: (i, 0)),
            pl.BlockSpec((None, 1, k), lambda j, i: (layer, 0, 0)),
            _single((None, k, tn), lambda j, i: (layer, 0, j)),
        ],
        out_specs=pl.BlockSpec((tm, tn), lambda j, i: (i, j)),
        scratch_shapes=[pltpu.VMEM((k, tn), _BF16)],
        compiler_params=_params("arbitrary", "arbitrary"),
        name=name,
    )(x, _vec3(g), w)


def _mixer_weight_specs(layer):
    l3 = lambda *_: (layer, 0, 0)
    l4 = lambda *_: (layer, 0, 0, 0)
    return [
        _single((None, len(POOL_WINDOWS), POOL_GW, POOL_GW), l4),
        pl.BlockSpec((None, 1, POOL_WIDTH), l3),
        pl.BlockSpec((None, CONF_K, CONV_WIDTH), l3),
        pl.BlockSpec((None, 1, CONV_WIDTH), l3),
        pl.BlockSpec((None, 1, CONV_WIDTH), l3),
        pl.BlockSpec((None, 1, CONV_WIDTH), l3),
        _single((None, CONV_WIDTH, CONV_WIDTH), l3),
        pl.BlockSpec((None, 1, CONV_WIDTH), l3),
        pl.BlockSpec((None, SC_K, SC_WIDTH), l3),
    ]


def _mixer_weight_args(w_pool, pool_scale, conf_dw, conf_dw_b, conf_ln_g, conf_ln_b,
                       conf_pw, conf_pw_b, sc_conv):
    return (w_pool, _vec3(pool_scale), conf_dw, _vec3(conf_dw_b), _vec3(conf_ln_g),
            _vec3(conf_ln_b), conf_pw, _vec3(conf_pw_b), sc_conv)


def _layernorm_silu(cb, lng_ref, lnb_ref):
    mu = jnp.mean(cb, axis=-1, keepdims=True)
    xc = cb - mu
    var = jnp.mean(xc * xc, axis=-1, keepdims=True)
    return _silu(xc * lax.rsqrt(var + EPS) * lng_ref[...] + lnb_ref[...])


def _conv31_block(eg_ref, dw_ref, c0, t0, rows):
    base = _HALO_CONF - (CONF_K - 1)
    acc = None
    for r in range(_SUBLANE):
        p = None
        for q in range((CONF_K + base) // _SUBLANE + 1):
            k = _SUBLANE * q + r - base
            if 0 <= k < CONF_K:
                lo = t0 + _SUBLANE * q
                term = dw_ref[k:k + 1, c0:c0 + _LANE] * eg_ref[lo:lo + rows + _SUBLANE, c0:c0 + _LANE]
                p = term if p is None else p + term
        shifted = p[r:r + rows, :]
        acc = shifted if acc is None else acc + shifted
    return acc


def _mixers_prompt(z_ref, wpool_ref, pscale_ref, dw_ref, dwb_ref, lng_ref, lnb_ref,
                   pwbf_ref, pwb_ref, scw_ref, y_ref, nconf_ref, nsc_ref,
                   eu_ref, eg_ref, ev_ref, cb_ref, *, tm, s, tick):
    tick()
    ev_ref[_HALO_SHORT:_HALO_SHORT + tm, :] = (
        z_ref[:, _O_SC:_O_SC + SC_WIDTH] * z_ref[:, _O_SH:_O_SH + SC_WIDTH])
    cv = scw_ref[0:1, :] * ev_ref[_HALO_SHORT - 2:_HALO_SHORT - 2 + tm, :]
    cv = cv + scw_ref[1:2, :] * ev_ref[_HALO_SHORT - 1:_HALO_SHORT - 1 + tm, :]
    cv = cv + scw_ref[2:3, :] * ev_ref[_HALO_SHORT:_HALO_SHORT + tm, :]
    y_ref[:, _Y_C:_Y_C + SC_WIDTH] = (z_ref[:, _O_SB:_O_SB + SC_WIDTH] * cv).astype(_BF16)
    nsc_ref[...] = ev_ref[tm + _HALO_SHORT - (SC_K - 1):tm + _HALO_SHORT, :]
    ev_ref[0:_HALO_SHORT, :] = ev_ref[tm:tm + _HALO_SHORT, :]

    eu_ref[_HALO_POOL:_HALO_POOL + tm, :] = z_ref[:, _O_U:_O_U + POOL_WIDTH]
    pos = s * tm + lax.broadcasted_iota(jnp.int32, (tm, 1), 0)
    for g, w in enumerate(POOL_WINDOWS):
        if g % 2 == 0:
            tick()
        c0 = g * POOL_GW
        u = eu_ref[_HALO_POOL:_HALO_POOL + tm, c0:c0 + POOL_GW]
        wsum = u
        for k in range(1, w):
            wsum = wsum + eu_ref[_HALO_POOL - k:_HALO_POOL - k + tm, c0:c0 + POOL_GW]
        cnt = jnp.minimum(w, pos + 1).astype(_F32)
        d = wsum / cnt - u
        y_ref[:, _Y_A + c0:_Y_A + c0 + POOL_GW] = (
            _dot(d, wpool_ref[g]) * pscale_ref[:, c0:c0 + POOL_GW]).astype(_BF16)
    eu_ref[0:_HALO_POOL, :] = eu_ref[tm:tm + _HALO_POOL, :]

    eg_ref[_HALO_CONF:_HALO_CONF + tm, :] = (
        z_ref[:, _O_GA:_O_GA + CONV_WIDTH] * jax.nn.sigmoid(z_ref[:, _O_GB:_O_GB + CONV_WIDTH]))
    blocks = [(t0, c0) for t0 in range(0, tm, _CONV_ROWS) for c0 in range(0, CONV_WIDTH, _LANE)]
    for n, (t0, c0) in enumerate(blocks):
        if n % _CONV_BLOCKS_PER_TICK == 0:
            tick()
        cb_ref[t0:t0 + _CONV_ROWS, c0:c0 + _LANE] = (
            _conv31_block(eg_ref, dw_ref, c0, t0, _CONV_ROWS) + dwb_ref[:, c0:c0 + _LANE])
    tick()
    act = _layernorm_silu(cb_ref[...], lng_ref, lnb_ref).astype(_BF16)
    y_ref[:, _Y_B:_Y_B + CONV_WIDTH] = (_dot(act, pwbf_ref[...]) + pwb_ref[...]).astype(_BF16)
    nconf_ref[...] = eg_ref[tm + _HALO_CONF - (CONF_K - 1):tm + _HALO_CONF, :]
    eg_ref[0:_HALO_CONF, :] = eg_ref[tm:tm + _HALO_CONF, :]


def _mix_out_prompt_kernel(z_ref, h_ref, wpool_ref, pscale_ref, dw_ref, dwb_ref, lng_ref, lnb_ref,
                           pw_ref, pwb_ref, scw_ref, wout_ref,
                           o_ref, nconf_ref, nsc_ref,
                           eu_ref, eg_ref, ev_ref, cb_ref, y0_ref, y1_ref, wo_ref, pwbf_ref,
                           *, tm, n_s, n_tiles):
    i = pl.program_id(0)
    s = jnp.minimum(i, n_tiles - 1) % n_s

    @pl.when(i == 0)
    def _():
        wo_ref[...] = wout_ref[...].astype(_BF16)
        pwbf_ref[...] = pw_ref[...].astype(_BF16)
        eg_ref[_HALO_CONF + tm:_HALO_CONF + tm + _SUBLANE, :] = jnp.zeros((_SUBLANE, CONV_WIDTH), _F32)
        y1_ref[...] = jnp.zeros(y1_ref.shape, _BF16)

    @pl.when(s == 0)
    def _():
        eu_ref[0:_HALO_POOL, :] = jnp.zeros((_HALO_POOL, POOL_WIDTH), _F32)
        eg_ref[0:_HALO_CONF, :] = jnp.zeros((_HALO_CONF, CONV_WIDTH), _F32)
        ev_ref[0:_HALO_SHORT, :] = jnp.zeros((_HALO_SHORT, SC_WIDTH), _F32)

    def step(y_prev_ref, y_cur_ref):
        pending = list(range(0, D_MODEL, _OUT_COLS_PER_TICK))

        def tick():
            if pending:
                c0 = pending.pop(0)
                cols = slice(c0, c0 + _OUT_COLS_PER_TICK)
                o_ref[:, cols] = h_ref[:, cols] + _dot(y_prev_ref[...], wo_ref[:, cols])

        _mixers_prompt(z_ref, wpool_ref, pscale_ref, dw_ref, dwb_ref, lng_ref, lnb_ref,
                       pwbf_ref, pwb_ref, scw_ref, y_cur_ref, nconf_ref, nsc_ref,
                       eu_ref, eg_ref, ev_ref, cb_ref, tm=tm, s=s, tick=tick)
        while pending:
            tick()

    @pl.when(i % 2 == 0)
    def _():
        step(y1_ref, y0_ref)

    @pl.when(i % 2 == 1)
    def _():
        step(y0_ref, y1_ref)


def _mix_out_prompt(z, h, nb, layer, mix_w, w_out, *, tm):
    m = z.shape[0]
    n_tiles = m // tm
    n_s = n_tiles // nb
    cur = lambda i: (jnp.minimum(i, n_tiles - 1), 0)
    prev = lambda i: (jnp.maximum(i - 1, 0), 0)
    seq = lambda i: (jnp.minimum(i, n_tiles - 1) // n_s, 0, 0)
    return pl.pallas_call(
        functools.partial(_mix_out_prompt_kernel, tm=tm, n_s=n_s, n_tiles=n_tiles),
        out_shape=(
            jax.ShapeDtypeStruct((m, D_MODEL), _F32),
            jax.ShapeDtypeStruct((nb, CONF_K - 1, CONV_WIDTH), _F32),
            jax.ShapeDtypeStruct((nb, SC_K - 1, SC_WIDTH), _F32),
        ),
        grid=(n_tiles + 1,),
        in_specs=([pl.BlockSpec((tm, IN_COLS), cur), pl.BlockSpec((tm, D_MODEL), prev)]
                  + _mixer_weight_specs(layer)
                  + [_single((None, D_MODEL, D_MODEL), lambda i: (layer, 0, 0))]),
        out_specs=(
            pl.BlockSpec((tm, D_MODEL), prev),
            pl.BlockSpec((None, CONF_K - 1, CONV_WIDTH), seq),
            pl.BlockSpec((None, SC_K - 1, SC_WIDTH), seq),
        ),
        scratch_shapes=[
            pltpu.VMEM((_HALO_POOL + tm, POOL_WIDTH), _F32),
            pltpu.VMEM((_HALO_CONF + tm + _SUBLANE, CONV_WIDTH), _F32),
            pltpu.VMEM((_HALO_SHORT + tm, SC_WIDTH), _F32),
            pltpu.VMEM((tm, CONV_WIDTH), _F32),
            pltpu.VMEM((tm, D_MODEL), _BF16),
            pltpu.VMEM((tm, D_MODEL), _BF16),
            pltpu.VMEM((D_MODEL, D_MODEL), _BF16),
            pltpu.VMEM((CONV_WIDTH, CONV_WIDTH), _BF16),
        ],
        compiler_params=_params("arbitrary"),
        name="mix_out_prompt",
    )(z, h, *_mixer_weight_args(*mix_w), w_out)


def _mix_out_sample_kernel(z_ref, h_ref, spool_ref, sconf_ref, ssc_ref,
                           wpool_ref, pscale_ref, dw_ref, dwb_ref, lng_ref, lnb_ref,
                           pw_ref, pwb_ref, scw_ref, wout_ref,
                           o_ref, g_ref, v_ref):
    out = h_ref[...]
    for g, w in enumerate(POOL_WINDOWS):
        c0 = g * POOL_GW
        u = z_ref[:, _O_U + c0:_O_U + c0 + POOL_GW]
        wsum = u + jnp.sum(spool_ref[:, POOL_BUF - (w - 1):POOL_BUF, c0:c0 + POOL_GW], axis=1)
        d = wsum / jnp.float32(min(w, PAST_LEN + 1)) - u
        y_a = _dot(d, wpool_ref[g]) * pscale_ref[:, c0:c0 + POOL_GW]
        out = out + _dot(y_a, wout_ref[_Y_A + c0:_Y_A + c0 + POOL_GW, :])

    gl = z_ref[:, _O_GA:_O_GA + CONV_WIDTH] * jax.nn.sigmoid(z_ref[:, _O_GB:_O_GB + CONV_WIDTH])
    g_ref[...] = gl
    acc = dw_ref[CONF_K - 1:CONF_K, :] * gl
    acc = acc + jnp.sum(sconf_ref[...] * dw_ref[0:CONF_K - 1, :][None], axis=1)
    y_b = _dot(_layernorm_silu(acc + dwb_ref[...], lng_ref, lnb_ref), pw_ref[...]) + pwb_ref[...]
    out = out + _dot(y_b, wout_ref[_Y_B:_Y_B + CONV_WIDTH, :])

    v = z_ref[:, _O_SC:_O_SC + SC_WIDTH] * z_ref[:, _O_SH:_O_SH + SC_WIDTH]
    v_ref[...] = v
    cv = jnp.sum(ssc_ref[...] * scw_ref[0:SC_K - 1, :][None], axis=1) + scw_ref[SC_K - 1:SC_K, :] * v
    y_c = z_ref[:, _O_SB:_O_SB + SC_WIDTH] * cv
    o_ref[...] = out + _dot(y_c, wout_ref[_Y_C:_Y_C + SC_WIDTH, :])


def _mix_out_sample(z, h, state_pool, state_conf, state_sc, layer, mix_w, w_out):
    m = z.shape[0]
    full = lambda a: pl.BlockSpec(a.shape, lambda i: (0,) * a.ndim)
    state = lambda a: _single((None,) + a.shape[1:], lambda i: (layer, 0, 0, 0))
    return pl.pallas_call(
        _mix_out_sample_kernel,
        out_shape=(
            jax.ShapeDtypeStruct((m, D_MODEL), _F32),
            jax.ShapeDtypeStruct((m, CONV_WIDTH), _F32),
            jax.ShapeDtypeStruct((m, SC_WIDTH), _F32),
        ),
        grid=(1,),
        in_specs=([full(z), full(h), state(state_pool), state(state_conf), state(state_sc)]
                  + _mixer_weight_specs(layer)
                  + [_single((None, D_MODEL, D_MODEL), lambda i: (layer, 0, 0))]),
        out_specs=(
            pl.BlockSpec((m, D_MODEL), lambda i: (0, 0)),
            pl.BlockSpec((m, CONV_WIDTH), lambda i: (0, 0)),
            pl.BlockSpec((m, SC_WIDTH), lambda i: (0, 0)),
        ),
        compiler_params=_params("arbitrary"),
        name="mix_out_sample",
    )(z, h, state_pool, state_conf, state_sc, *_mixer_weight_args(*mix_w), w_out)


def _ffn_kernel(h_ref, hs_ref, g_ref, k_ref, st_ref, wup_hbm, wdn_hbm,
                o_ref, os_ref, nf_ref, us_ref,
                hn_ref, ea_ref, eb_ref, ca_ref, cb_ref, wa_buf, wb_buf, wd_buf, sem,
                *, layer, tm, ts, sub, tf, n_c, n_s, n_tiles):
    i = pl.program_id(0)
    s = i % n_s

    def weight_copies(chunk, slot):
        col_a = pl.multiple_of(chunk * tf, tf)
        col_b = pl.multiple_of(D_FF + chunk * tf, tf)
        return (
            pltpu.make_async_copy(wup_hbm.at[layer, :, pl.ds(col_a, tf)], wa_buf.at[slot], sem.at[0, slot]),
            pltpu.make_async_copy(wup_hbm.at[layer, :, pl.ds(col_b, tf)], wb_buf.at[slot], sem.at[1, slot]),
            pltpu.make_async_copy(wdn_hbm.at[layer, pl.ds(col_a, tf), :], wd_buf.at[slot], sem.at[2, slot]),
        )

    @pl.when(i == 0)
    def _():
        for cp in weight_copies(0, 0):
            cp.start()

    h = h_ref[...]
    hs = hs_ref[...]
    hn_ref[0:tm, :] = _rms(h, g_ref[...]).astype(_BF16)
    hn_ref[tm:tm + ts, :] = _rms(hs, g_ref[...]).astype(_BF16)
    o_ref[...] = h
    os_ref[...] = hs

    @pl.when(s == 0)
    def _():
        ca_ref[...] = jnp.zeros(ca_ref.shape, _F32)
        cb_ref[...] = jnp.zeros(cb_ref.shape, _F32)

    starts = list(range(0, tm, sub))
    blocks = [(r0, r0 + sub) for r0 in starts[:-1]] + [(starts[-1], tm + ts)]

    def conv(e_ref, k, r0, r1):
        lo, n = _HALO_SHORT + r0, r1 - r0
        out = k[0:1, :] * e_ref[lo - 2:lo - 2 + n, :]
        out = out + k[1:2, :] * e_ref[lo - 1:lo - 1 + n, :]
        return out + k[2:3, :] * e_ref[lo:lo + n, :]

    def conv_sample(e_ref, k, col):
        new = e_ref[_HALO_SHORT + tm:_HALO_SHORT + tm + ts, :]
        older = st_ref[:, pl.ds(col, tf)]
        newer = st_ref[:, pl.ds(pl.multiple_of(2 * D_FF + col, tf), tf)]
        return k[0:1, :] * older + k[1:2, :] * newer + k[2:3, :] * new

    def chunk(c, carry):
        step = i * n_c + c
        slot = step % 2
        for cp in weight_copies(c, slot):
            cp.wait()

        @pl.when(step + 1 < n_tiles * n_c)
        def _():
            for cp in weight_copies(jnp.where(c + 1 == n_c, 0, c + 1), 1 - slot):
                cp.start()

        col_a = pl.multiple_of(c * tf, tf)
        col_b = pl.multiple_of(D_FF + c * tf, tf)
        wa = wa_buf[slot]
        wb = wb_buf[slot]
        wd = wd_buf[slot]
        ka = k_ref[:, pl.ds(col_a, tf)]
        kb = k_ref[:, pl.ds(col_b, tf)]
        ea_ref[0:_HALO_SHORT, :] = ca_ref[c]
        eb_ref[0:_HALO_SHORT, :] = cb_ref[c]

        for r0, r1 in blocks:
            hn = hn_ref[r0:r1, :]
            ea_ref[_HALO_SHORT + r0:_HALO_SHORT + r1, :] = _dot(hn, wa)
            eb_ref[_HALO_SHORT + r0:_HALO_SHORT + r1, :] = _dot(hn, wb)

        for r0, r1 in blocks:
            r1p = min(r1, tm)
            act = _silu(conv(ea_ref, ka, r0, r1p)) * conv(eb_ref, kb, r0, r1p)
            if r1 > tm:
                act_s = _silu(conv_sample(ea_ref, ka, col_a)) * conv_sample(eb_ref, kb, col_b)
                res = _dot(jnp.concatenate([act, act_s], axis=0).astype(_BF16), wd)
                o_ref[r0:tm, :] += res[0:tm - r0, :]
                os_ref[...] += res[tm - r0:tm - r0 + ts, :]
            else:
                o_ref[r0:r1, :] += _dot(act.astype(_BF16), wd)

        ca_ref[c] = ea_ref[tm:tm + _HALO_SHORT, :]
        cb_ref[c] = eb_ref[tm:tm + _HALO_SHORT, :]
        tail = slice(tm + _HALO_SHORT - (FFN_K - 1), tm + _HALO_SHORT)
        nf_ref[:, pl.ds(col_a, tf)] = ea_ref[tail, :]
        nf_ref[:, pl.ds(col_b, tf)] = eb_ref[tail, :]
        new = slice(_HALO_SHORT + tm, _HALO_SHORT + tm + ts)
        us_ref[:, pl.ds(col_a, tf)] = ea_ref[new, :]
        us_ref[:, pl.ds(col_b, tf)] = eb_ref[new, :]
        return carry

    lax.fori_loop(0, n_c, chunk, 0)


def _ffn(h, hs, nb, norm_g, w_up, ffn_conv, w_down, state_ffn, layer, *, tm, tf, sub):
    m = h.shape[0]
    n_tiles = m // tm
    n_s = n_tiles // nb
    n_c = D_FF // tf
    ts = hs.shape[0] // n_tiles
    assert ts * n_tiles == hs.shape[0] and ts % _BF16_ROWS == 0, (hs.shape, n_tiles)
    n_hist = state_ffn.shape[2]
    state_flat = state_ffn.reshape(state_ffn.shape[0], state_ffn.shape[1], n_hist * 2 * D_FF)
    return pl.pallas_call(
        functools.partial(_ffn_kernel, layer=layer, tm=tm, ts=ts, sub=sub, tf=tf, n_c=n_c,
                          n_s=n_s, n_tiles=n_tiles),
        out_shape=(
            jax.ShapeDtypeStruct((m, D_MODEL), _F32),
            jax.ShapeDtypeStruct(hs.shape, _F32),
            jax.ShapeDtypeStruct((n_tiles, FFN_K - 1, 2 * D_FF), _F32),
            jax.ShapeDtypeStruct((hs.shape[0], 2 * D_FF), _F32),
        ),
        grid=(n_tiles,),
        in_specs=[
            _single((tm, D_MODEL), lambda i: (i, 0)),
            pl.BlockSpec((ts, D_MODEL), lambda i: (i, 0)),
            pl.BlockSpec((None, 1, D_MODEL), lambda i: (layer, 0, 0)),
            pl.BlockSpec((None, FFN_K, 2 * D_FF), lambda i: (layer, 0, 0)),
            pl.BlockSpec((None, ts, n_hist * 2 * D_FF), lambda i: (layer, i, 0)),
            pl.BlockSpec(memory_space=pl.ANY),
            pl.BlockSpec(memory_space=pl.ANY),
        ],
        out_specs=(
            pl.BlockSpec((tm, D_MODEL), lambda i: (i, 0)),
            pl.BlockSpec((ts, D_MODEL), lambda i: (i, 0)),
            pl.BlockSpec((None, FFN_K - 1, 2 * D_FF), lambda i: (i, 0, 0)),
            pl.BlockSpec((ts, 2 * D_FF), lambda i: (i, 0)),
        ),
        scratch_shapes=[
            pltpu.VMEM((tm + ts, D_MODEL), _BF16),
            pltpu.VMEM((_HALO_SHORT + tm + ts, tf), _F32),
            pltpu.VMEM((_HALO_SHORT + tm + ts, tf), _F32),
            pltpu.VMEM((n_c, _HALO_SHORT, tf), _F32),
            pltpu.VMEM((n_c, _HALO_SHORT, tf), _F32),
            pltpu.VMEM((2, D_MODEL, tf), _BF16),
            pltpu.VMEM((2, D_MODEL, tf), _BF16),
            pltpu.VMEM((2, tf, D_MODEL), _BF16),
            pltpu.SemaphoreType.DMA((3, 2)),
        ],
        compiler_params=_params("arbitrary"),
        name="ffn",
    )(h, hs, _vec3(norm_g), ffn_conv, state_flat, w_up, w_down)


def _ple_kernel(h_ref, g_ref, gate_ref, p_ref, proj_ref, gf_ref, o_ref, gate_bf, proj_bf,
                *, final, tm, sub):
    @pl.when(pl.program_id(0) == 0)
    def _():
        gate_bf[...] = gate_ref[...].astype(_BF16)
        proj_bf[...] = proj_ref[...].astype(_BF16)

    for r0 in range(0, tm, sub):
        h = h_ref[r0:r0 + sub, :]
        hn = _rms(h, g_ref[...]).astype(_BF16)
        gate = jax.nn.sigmoid(_dot(hn, gate_bf[...]))
        out = h + _dot(p_ref[r0:r0 + sub, :].astype(_BF16), proj_bf[...]) * gate
        if final:
            out = _rms(out, gf_ref[...])
        o_ref[r0:r0 + sub, :] = out


def _ple(h, norm_g, gate_w, p, proj_w, layer, final_g, *, tm, sub, final, name):
    m = h.shape[0]
    return pl.pallas_call(
        functools.partial(_ple_kernel, final=final, tm=tm, sub=sub),
        out_shape=jax.ShapeDtypeStruct((m, D_MODEL), _F32),
        grid=(m // tm,),
        in_specs=[
            pl.BlockSpec((tm, D_MODEL), lambda i: (i, 0)),
            pl.BlockSpec((None, 1, D_MODEL), lambda i: (layer, 0, 0)),
            _single((None, D_MODEL, D_MODEL), lambda i: (layer, 0, 0)),
            pl.BlockSpec((None, tm, PLE_DIM), lambda i: (layer, i, 0)),
            _single((None, PLE_DIM, D_MODEL), lambda i: (layer, 0, 0)),
            pl.BlockSpec((1, D_MODEL), lambda i: (0, 0)),
        ],
        out_specs=pl.BlockSpec((tm, D_MODEL), lambda i: (i, 0)),
        scratch_shapes=[
            pltpu.VMEM((D_MODEL, D_MODEL), _BF16),
            pltpu.VMEM((PLE_DIM, D_MODEL), _BF16),
        ],
        compiler_params=_params("arbitrary"),
        name=name,
    )(h, _vec3(norm_g), gate_w, p, proj_w, final_g.reshape(1, -1))


def kernel(x_prompt, x_sample, p_prompt, p_sample, state_pool, state_conf, state_sc, state_ffn,
           norm_mix, w_in, w_pool, pool_scale, conf_dw, conf_dw_b, conf_ln_g, conf_ln_b,
           conf_pw, conf_pw_b, sc_conv, w_out, norm_ffn, w_up, ffn_conv, w_down,
           norm_ple, ple_gate, ple_proj, norm_final):
    nb, seq, _ = x_prompt.shape
    ns = x_sample.shape[0]
    depth = w_in.shape[0]
    hp = x_prompt.reshape(nb * seq, D_MODEL)
    hs = x_sample.reshape(ns, D_MODEL)
    pp = p_prompt.reshape(depth, nb * seq, PLE_DIM)
    ps = p_sample.reshape(depth, ns, PLE_DIM)
    mix_w = (w_pool, pool_scale, conf_dw, conf_dw_b, conf_ln_g, conf_ln_b, conf_pw, conf_pw_b, sc_conv)

    w_up_bf = w_up.astype(_BF16)
    w_down_bf = w_down.astype(_BF16)

    prompt_states = [[] for _ in range(4)]
    sample_rows = [[] for _ in range(4)]
    for i in range(depth):
        last = i == depth - 1

        z = _norm_matmul(hp, norm_mix, w_in, i, tm=_TM_IN, tn=_TN_IN, sub=_SUB_IN,
                         name="in_proj_prompt")
        npool_p = z.reshape(nb, seq, IN_COLS)[:, seq - POOL_BUF:, :POOL_WIDTH]
        h1, nconf_p, nsc_p = _mix_out_prompt(z, hp, nb, i, mix_w, w_out, tm=_TM_MIX)
        zs = _norm_matmul(hs, norm_mix, w_in, i, tm=ns, tn=_TN_IN, sub=ns, name="in_proj_sample")
        h1s, g_s, v_s = _mix_out_sample(zs, hs, state_pool, state_conf, state_sc, i, mix_w, w_out)

        h2, h2s, nf, up_s = _ffn(h1, h1s, nb, norm_ffn, w_up_bf, ffn_conv, w_down_bf,
                                 state_ffn, i, tm=_TM_FFN, tf=_TF_FFN, sub=_SUB_FFN)
        tiles_per_seq = nf.shape[0] // nb
        nffn_p = nf[tiles_per_seq - 1::tiles_per_seq]

        hp = _ple(h2, norm_ple, ple_gate, pp, ple_proj, i, norm_final,
                  tm=_TM_PLE, sub=_SUB_PLE, final=last, name="ple_prompt")
        hs = _ple(h2s, norm_ple, ple_gate, ps, ple_proj, i, norm_final,
                  tm=ns, sub=ns, final=last, name="ple_sample")

        for lst, val in zip(prompt_states, (npool_p, nconf_p, nsc_p, nffn_p)):
            lst.append(val)
        for lst, val in zip(sample_rows, (zs[:, :POOL_WIDTH], g_s, v_s, up_s)):
            lst.append(val)

    def shifted(state, rows):
        return jnp.concatenate([state[:, :, 1:], jnp.stack(rows)[:, :, None]], axis=2)

    new_p = [jnp.stack(l) for l in prompt_states]
    new_s = [shifted(st, rows) for st, rows in
             zip((state_pool, state_conf, state_sc, state_ffn), sample_rows)]
    y_prompt = hp.reshape(nb, seq, D_MODEL)
    y_sample = hs.reshape(ns, 1, D_MODEL)
    return (y_prompt, y_sample, new_p[0], new_s[0], new_p[1], new_s[1],
            new_p[2], new_s[2], new_p[3], new_s[3])
```

```python
import functools

import jax
import jax.numpy as jnp
from jax import lax
from jax.experimental import pallas as pl
from jax.experimental.pallas import tpu as pltpu

D_MODEL = 2048
POOL_WIDTH = 512
POOL_WINDOWS = (2, 4, 8, 16)
POOL_GW = 128
POOL_BUF = 15
CONV_WIDTH = 768
CONF_K = 31
SC_WIDTH = 768
SC_K = 3
FFN_K = 3
D_FF = 5632
PLE_DIM = 256
IN_COLS = POOL_WIDTH + 2 * CONV_WIDTH + 3 * SC_WIDTH
EPS = 1e-6
PAST_LEN = 16384

_O_U = 0
_O_GA = _O_U + POOL_WIDTH
_O_GB = _O_GA + CONV_WIDTH
_O_SB = _O_GB + CONV_WIDTH
_O_SC = _O_SB + SC_WIDTH
_O_SH = _O_SC + SC_WIDTH
_Y_A = 0
_Y_B = POOL_WIDTH
_Y_C = POOL_WIDTH + CONV_WIDTH

_VMEM_LIMIT = 56 * 1024 * 1024
_LANE = 128
_SUBLANE = 8
_BF16_ROWS = 16

_HALO_POOL = 16
_HALO_CONF = 32
_HALO_SHORT = 8

_TM_IN, _TN_IN, _SUB_IN = 512, IN_COLS // 2, 256
_TM_MIX = 256
_CONV_ROWS = 128
_CONV_BLOCKS_PER_TICK = 3
_OUT_COLS_PER_TICK = 256
_TM_FFN, _TF_FFN, _SUB_FFN = 1024, 512, 512
_TM_PLE, _SUB_PLE = 512, 256

_BF16 = jnp.bfloat16
_F32 = jnp.float32


def _params(*sem):
    return pltpu.CompilerParams(dimension_semantics=sem, vmem_limit_bytes=_VMEM_LIMIT)


def _rms(x, g):
    ms = jnp.mean(x * x, axis=-1, keepdims=True)
    return x * lax.rsqrt(ms + EPS) * g


def _dot(a, b):
    return jnp.dot(a, b, preferred_element_type=_F32)


def _silu(x):
    return x * jax.nn.sigmoid(x)


def _single(block, index_map):
    return pl.BlockSpec(block, index_map, pipeline_mode=pl.Buffered(1))


def _vec3(a):
    return a.reshape(a.shape[0], 1, a.shape[1])


def _norm_matmul_kernel(x_ref, xs_ref, g_ref, w_ref, o_ref, os_ref, wb_ref, *, tm, sub, n_i):
    i = pl.program_id(1)

    @pl.when(i == 0)
    def _():
        wb_ref[...] = w_ref[...].astype(_BF16)

    @pl.when(i < n_i)
    def _():
        for r0 in range(0, tm, sub):
            hn = _rms(x_ref[r0:r0 + sub, :], g_ref[...]).astype(_BF16)
            o_ref[r0:r0 + sub, :] = _dot(hn, wb_ref[...])

    @pl.when(i == n_i)
    def _():
        os_ref[...] = _dot(_rms(xs_ref[...], g_ref[...]).astype(_BF16), wb_ref[...])


def _norm_matmul(x, xs, g, w, layer, *, tm, tn, sub):
    m, k = x.shape
    ms = xs.shape[0]
    n = w.shape[2]
    n_i = m // tm
    row = lambda j, i: (jnp.minimum(i, n_i - 1), 0)
    return pl.pallas_call(
        functools.partial(_norm_matmul_kernel, tm=tm, sub=sub, n_i=n_i),
        out_shape=(jax.ShapeDtypeStruct((m, n), _F32), jax.ShapeDtypeStruct((ms, n), _F32)),
        grid=(n // tn, n_i + 1),
        in_specs=[
            pl.BlockSpec((tm, k), row),
            pl.BlockSpec((ms, k), lambda j, i: (0, 0)),
            pl.BlockSpec((None, 1, k), lambda j, i: (layer, 0, 0)),
            _single((None, k, tn), lambda j, i: (layer, 0, j)),
        ],
        out_specs=(
            pl.BlockSpec((tm, tn), lambda j, i: (jnp.minimum(i, n_i - 1), j)),
            pl.BlockSpec((ms, tn), lambda j, i: (0, j)),
        ),
        scratch_shapes=[pltpu.VMEM((k, tn), _BF16)],
        compiler_params=_params("arbitrary", "arbitrary"),
        name="in_proj",
    )(x, xs, _vec3(g), w)


def _mixer_weight_specs(layer):
    l3 = lambda *_: (layer, 0, 0)
    l4 = lambda *_: (layer, 0, 0, 0)
    return [
        _single((None, len(POOL_WINDOWS), POOL_GW, POOL_GW), l4),
        pl.BlockSpec((None, 1, POOL_WIDTH), l3),
        pl.BlockSpec((None, CONF_K, CONV_WIDTH), l3),
        pl.BlockSpec((None, 1, CONV_WIDTH), l3),
        pl.BlockSpec((None, 1, CONV_WIDTH), l3),
        pl.BlockSpec((None, 1, CONV_WIDTH), l3),
        _single((None, CONV_WIDTH, CONV_WIDTH), l3),
        pl.BlockSpec((None, 1, CONV_WIDTH), l3),
        pl.BlockSpec((None, SC_K, SC_WIDTH), l3),
    ]


def _mixer_weight_args(w_pool, pool_scale, conf_dw, conf_dw_b, conf_ln_g, conf_ln_b,
                       conf_pw, conf_pw_b, sc_conv):
    return (w_pool, _vec3(pool_scale), conf_dw, _vec3(conf_dw_b), _vec3(conf_ln_g),
            _vec3(conf_ln_b), conf_pw, _vec3(conf_pw_b), sc_conv)


def _layernorm_silu(cb, lng_ref, lnb_ref):
    mu = jnp.mean(cb, axis=-1, keepdims=True)
    xc = cb - mu
    var = jnp.mean(xc * xc, axis=-1, keepdims=True)
    return _silu(xc * lax.rsqrt(var + EPS) * lng_ref[...] + lnb_ref[...])


def _conv31_block(eg_ref, dw_ref, c0, t0, rows):
    base = _HALO_CONF - (CONF_K - 1)
    acc = None
    for r in range(_SUBLANE):
        p = None
        for q in range((CONF_K + base) // _SUBLANE + 1):
            k = _SUBLANE * q + r - base
            if 0 <= k < CONF_K:
                lo = t0 + _SUBLANE * q
                term = dw_ref[k:k + 1, c0:c0 + _LANE] * eg_ref[lo:lo + rows + _SUBLANE, c0:c0 + _LANE]
                p = term if p is None else p + term
        shifted = p[r:r + rows, :]
        acc = shifted if acc is None else acc + shifted
    return acc


def _mixers_prompt(z_ref, wpool_ref, pscale_ref, dw_ref, dwb_ref, lng_ref, lnb_ref,
                   pwbf_ref, pwb_ref, scw_ref, y_ref, nconf_ref, nsc_ref,
                   eu_ref, eg_ref, ev_ref, cb_ref, *, tm, s, tick):
    tick()
    ev_ref[_HALO_SHORT:_HALO_SHORT + tm, :] = (
        z_ref[:, _O_SC:_O_SC + SC_WIDTH] * z_ref[:, _O_SH:_O_SH + SC_WIDTH])
    cv = scw_ref[0:1, :] * ev_ref[_HALO_SHORT - 2:_HALO_SHORT - 2 + tm, :]
    cv = cv + scw_ref[1:2, :] * ev_ref[_HALO_SHORT - 1:_HALO_SHORT - 1 + tm, :]
    cv = cv + scw_ref[2:3, :] * ev_ref[_HALO_SHORT:_HALO_SHORT + tm, :]
    y_ref[:, _Y_C:_Y_C + SC_WIDTH] = (z_ref[:, _O_SB:_O_SB + SC_WIDTH] * cv).astype(_BF16)
    nsc_ref[...] = ev_ref[tm + _HALO_SHORT - (SC_K - 1):tm + _HALO_SHORT, :]
    ev_ref[0:_HALO_SHORT, :] = ev_ref[tm:tm + _HALO_SHORT, :]

    eu_ref[_HALO_POOL:_HALO_POOL + tm, :] = z_ref[:, _O_U:_O_U + POOL_WIDTH]
    pos = s * tm + lax.broadcasted_iota(jnp.int32, (tm, 1), 0)
    for g, w in enumerate(POOL_WINDOWS):
        if g % 2 == 0:
            tick()
        c0 = g * POOL_GW
        u = eu_ref[_HALO_POOL:_HALO_POOL + tm, c0:c0 + POOL_GW]
        wsum = u
        for k in range(1, w):
            wsum = wsum + eu_ref[_HALO_POOL - k:_HALO_POOL - k + tm, c0:c0 + POOL_GW]
        cnt = jnp.minimum(w, pos + 1).astype(_F32)
        d = wsum / cnt - u
        y_ref[:, _Y_A + c0:_Y_A + c0 + POOL_GW] = (
            _dot(d, wpool_ref[g]) * pscale_ref[:, c0:c0 + POOL_GW]).astype(_BF16)
    eu_ref[0:_HALO_POOL, :] = eu_ref[tm:tm + _HALO_POOL, :]

    eg_ref[_HALO_CONF:_HALO_CONF + tm, :] = (
        z_ref[:, _O_GA:_O_GA + CONV_WIDTH] * jax.nn.sigmoid(z_ref[:, _O_GB:_O_GB + CONV_WIDTH]))
    blocks = [(t0, c0) for t0 in range(0, tm, _CONV_ROWS) for c0 in range(0, CONV_WIDTH, _LANE)]
    for n, (t0, c0) in enumerate(blocks):
        if n % _CONV_BLOCKS_PER_TICK == 0:
            tick()
        cb_ref[t0:t0 + _CONV_ROWS, c0:c0 + _LANE] = (
            _conv31_block(eg_ref, dw_ref, c0, t0, _CONV_ROWS) + dwb_ref[:, c0:c0 + _LANE])
    tick()
    act = _layernorm_silu(cb_ref[...], lng_ref, lnb_ref).astype(_BF16)
    y_ref[:, _Y_B:_Y_B + CONV_WIDTH] = (_dot(act, pwbf_ref[...]) + pwb_ref[...]).astype(_BF16)
    nconf_ref[...] = eg_ref[tm + _HALO_CONF - (CONF_K - 1):tm + _HALO_CONF, :]
    eg_ref[0:_HALO_CONF, :] = eg_ref[tm:tm + _HALO_CONF, :]


def _mix_out_prompt_kernel(z_ref, h_ref, wpool_ref, pscale_ref, dw_ref, dwb_ref, lng_ref, lnb_ref,
                           pw_ref, pwb_ref, scw_ref, wout_ref,
                           o_ref, nconf_ref, nsc_ref,
                           eu_ref, eg_ref, ev_ref, cb_ref, y0_ref, y1_ref, wo_ref, pwbf_ref,
                           *, tm, n_s, n_tiles):
    i = pl.program_id(0)
    s = jnp.minimum(i, n_tiles - 1) % n_s

    @pl.when(i == 0)
    def _():
        wo_ref[...] = wout_ref[...].astype(_BF16)
        pwbf_ref[...] = pw_ref[...].astype(_BF16)
        eg_ref[_HALO_CONF + tm:_HALO_CONF + tm + _SUBLANE, :] = jnp.zeros((_SUBLANE, CONV_WIDTH), _F32)
        y1_ref[...] = jnp.zeros(y1_ref.shape, _BF16)

    @pl.when(s == 0)
    def _():
        eu_ref[0:_HALO_POOL, :] = jnp.zeros((_HALO_POOL, POOL_WIDTH), _F32)
        eg_ref[0:_HALO_CONF, :] = jnp.zeros((_HALO_CONF, CONV_WIDTH), _F32)
        ev_ref[0:_HALO_SHORT, :] = jnp.zeros((_HALO_SHORT, SC_WIDTH), _F32)

    def step(y_prev_ref, y_cur_ref):
        pending = list(range(0, D_MODEL, _OUT_COLS_PER_TICK))

        def tick():
            if pending:
                c0 = pending.pop(0)
                cols = slice(c0, c0 + _OUT_COLS_PER_TICK)
                o_ref[:, cols] = h_ref[:, cols] + _dot(y_prev_ref[...], wo_ref[:, cols])

        _mixers_prompt(z_ref, wpool_ref, pscale_ref, dw_ref, dwb_ref, lng_ref, lnb_ref,
                       pwbf_ref, pwb_ref, scw_ref, y_cur_ref, nconf_ref, nsc_ref,
                       eu_ref, eg_ref, ev_ref, cb_ref, tm=tm, s=s, tick=tick)
        while pending:
            tick()

    @pl.when(i % 2 == 0)
    def _():
        step(y1_ref, y0_ref)

    @pl.when(i % 2 == 1)
    def _():
        step(y0_ref, y1_ref)


def _mix_out_prompt(z, h, nb, layer, mix_w, w_out, *, tm):
    m = z.shape[0]
    n_tiles = m // tm
    n_s = n_tiles // nb
    cur = lambda i: (jnp.minimum(i, n_tiles - 1), 0)
    prev = lambda i: (jnp.maximum(i - 1, 0), 0)
    seq = lambda i: (jnp.minimum(i, n_tiles - 1) // n_s, 0, 0)
    return pl.pallas_call(
        functools.partial(_mix_out_prompt_kernel, tm=tm, n_s=n_s, n_tiles=n_tiles),
        out_shape=(
            jax.ShapeDtypeStruct((m, D_MODEL), _F32),
            jax.ShapeDtypeStruct((nb, CONF_K - 1, CONV_WIDTH), _F32),
            jax.ShapeDtypeStruct((nb, SC_K - 1, SC_WIDTH), _F32),
        ),
        grid=(n_tiles + 1,),
        in_specs=([pl.BlockSpec((tm, IN_COLS), cur), pl.BlockSpec((tm, D_MODEL), prev)]
                  + _mixer_weight_specs(layer)
                  + [_single((None, D_MODEL, D_MODEL), lambda i: (layer, 0, 0))]),
        out_specs=(
            pl.BlockSpec((tm, D_MODEL), prev),
            pl.BlockSpec((None, CONF_K - 1, CONV_WIDTH), seq),
            pl.BlockSpec((None, SC_K - 1, SC_WIDTH), seq),
        ),
        scratch_shapes=[
            pltpu.VMEM((_HALO_POOL + tm, POOL_WIDTH), _F32),
            pltpu.VMEM((_HALO_CONF + tm + _SUBLANE, CONV_WIDTH), _F32),
            pltpu.VMEM((_HALO_SHORT + tm, SC_WIDTH), _F32),
            pltpu.VMEM((tm, CONV_WIDTH), _F32),
            pltpu.VMEM((tm, D_MODEL), _BF16),
            pltpu.VMEM((tm, D_MODEL), _BF16),
            pltpu.VMEM((D_MODEL, D_MODEL), _BF16),
            pltpu.VMEM((CONV_WIDTH, CONV_WIDTH), _BF16),
        ],
        compiler_params=_params("arbitrary"),
        name="mix_out_prompt",
    )(z, h, *_mixer_weight_args(*mix_w), w_out)


def _mix_out_sample_kernel(z_ref, h_ref, spool_ref, sconf_ref, ssc_ref,
                           wpool_ref, pscale_ref, dw_ref, dwb_ref, lng_ref, lnb_ref,
                           pw_ref, pwb_ref, scw_ref, wout_ref,
                           o_ref, g_ref, v_ref):
    out = h_ref[...]
    for g, w in enumerate(POOL_WINDOWS):
        c0 = g * POOL_GW
        u = z_ref[:, _O_U + c0:_O_U + c0 + POOL_GW]
        wsum = u + jnp.sum(spool_ref[:, POOL_BUF - (w - 1):POOL_BUF, c0:c0 + POOL_GW], axis=1)
        d = wsum / jnp.float32(min(w, PAST_LEN + 1)) - u
        y_a = _dot(d, wpool_ref[g]) * pscale_ref[:, c0:c0 + POOL_GW]
        out = out + _dot(y_a, wout_ref[_Y_A + c0:_Y_A + c0 + POOL_GW, :])

    gl = z_ref[:, _O_GA:_O_GA + CONV_WIDTH] * jax.nn.sigmoid(z_ref[:, _O_GB:_O_GB + CONV_WIDTH])
    g_ref[...] = gl
    acc = dw_ref[CONF_K - 1:CONF_K, :] * gl
    acc = acc + jnp.sum(sconf_ref[...] * dw_ref[0:CONF_K - 1, :][None], axis=1)
    y_b = _dot(_layernorm_silu(acc + dwb_ref[...], lng_ref, lnb_ref), pw_ref[...]) + pwb_ref[...]
    out = out + _dot(y_b, wout_ref[_Y_B:_Y_B + CONV_WIDTH, :])

    v = z_ref[:, _O_SC:_O_SC + SC_WIDTH] * z_ref[:, _O_SH:_O_SH + SC_WIDTH]
    v_ref[...] = v
    cv = jnp.sum(ssc_ref[...] * scw_ref[0:SC_K - 1, :][None], axis=1) + scw_ref[SC_K - 1:SC_K, :] * v
    y_c = z_ref[:, _O_SB:_O_SB + SC_WIDTH] * cv
    o_ref[...] = out + _dot(y_c, wout_ref[_Y_C:_Y_C + SC_WIDTH, :])


def _mix_out_sample(z, h, state_pool, state_conf, state_sc, layer, mix_w, w_out):
    m = z.shape[0]
    full = lambda a: pl.BlockSpec(a.shape, lambda i: (0,) * a.ndim)
    state = lambda a: _single((None,) + a.shape[1:], lambda i: (layer, 0, 0, 0))
    return pl.pallas_call(
        _mix_out_sample_kernel,
        out_shape=(
            jax.ShapeDtypeStruct((m, D_MODEL), _F32),
            jax.ShapeDtypeStruct((m, CONV_WIDTH), _F32),
            jax.ShapeDtypeStruct((m, SC_WIDTH), _F32),
        ),
        grid=(1,),
        in_specs=([full(z), full(h), state(state_pool), state(state_conf), state(state_sc)]
                  + _mixer_weight_specs(layer)
                  + [_single((None, D_MODEL, D_MODEL), lambda i: (layer, 0, 0))]),
        out_specs=(
            pl.BlockSpec((m, D_MODEL), lambda i: (0, 0)),
            pl.BlockSpec((m, CONV_WIDTH), lambda i: (0, 0)),
            pl.BlockSpec((m, SC_WIDTH), lambda i: (0, 0)),
        ),
        compiler_params=_params("arbitrary"),
        name="mix_out_sample",
    )(z, h, state_pool, state_conf, state_sc, *_mixer_weight_args(*mix_w), w_out)


def _ffn_kernel(h_ref, hs_ref, g_ref, k_ref, st_hbm, wup_hbm, wdn_hbm,
                o_ref, os_ref, nf_ref, us_ref,
                hn_ref, ea_ref, eb_ref, ca_ref, cb_ref, wa_buf, wb_buf, wd_buf, sa_buf, sb_buf, sem,
                *, layer, tm, ts, sub, tf, n_c, n_s, n_tiles):
    i = pl.program_id(0)
    s = i % n_s

    def chunk_copies(tile, chunk, slot):
        col_a = pl.multiple_of(chunk * tf, tf)
        col_b = pl.multiple_of(D_FF + chunk * tf, tf)
        rows = pl.ds(pl.multiple_of(tile * ts, ts), ts)
        return (
            pltpu.make_async_copy(wup_hbm.at[layer, :, pl.ds(col_a, tf)], wa_buf.at[slot], sem.at[0, slot]),
            pltpu.make_async_copy(wup_hbm.at[layer, :, pl.ds(col_b, tf)], wb_buf.at[slot], sem.at[1, slot]),
            pltpu.make_async_copy(wdn_hbm.at[layer, pl.ds(col_a, tf), :], wd_buf.at[slot], sem.at[2, slot]),
            pltpu.make_async_copy(st_hbm.at[layer, rows, :, pl.ds(col_a, tf)], sa_buf.at[slot], sem.at[3, slot]),
            pltpu.make_async_copy(st_hbm.at[layer, rows, :, pl.ds(col_b, tf)], sb_buf.at[slot], sem.at[4, slot]),
        )

    @pl.when(i == 0)
    def _():
        for cp in chunk_copies(0, 0, 0):
            cp.start()

    h = h_ref[...]
    hs = hs_ref[...]
    hn_ref[0:tm, :] = _rms(h, g_ref[...]).astype(_BF16)
    hn_ref[tm:tm + ts, :] = _rms(hs, g_ref[...]).astype(_BF16)
    o_ref[...] = h
    os_ref[...] = hs

    @pl.when(s == 0)
    def _():
        ca_ref[...] = jnp.zeros(ca_ref.shape, _F32)
        cb_ref[...] = jnp.zeros(cb_ref.shape, _F32)

    starts = list(range(0, tm, sub))
    blocks = [(r0, r0 + sub) for r0 in starts[:-1]] + [(starts[-1], tm + ts)]

    def conv(e_ref, k, r0, r1):
        lo, n = _HALO_SHORT + r0, r1 - r0
        out = k[0:1, :] * e_ref[lo - 2:lo - 2 + n, :]
        out = out + k[1:2, :] * e_ref[lo - 1:lo - 1 + n, :]
        return out + k[2:3, :] * e_ref[lo:lo + n, :]

    def conv_sample(e_ref, k, st):
        new = e_ref[_HALO_SHORT + tm:_HALO_SHORT + tm + ts, :]
        hist = jnp.sum(st * k[0:FFN_K - 1, :][None], axis=1)
        return hist + k[FFN_K - 1:FFN_K, :] * new

    def chunk(c, carry):
        step = i * n_c + c
        slot = step % 2
        for cp in chunk_copies(i, c, slot):
            cp.wait()

        @pl.when(step + 1 < n_tiles * n_c)
        def _():
            wrap = c + 1 == n_c
            for cp in chunk_copies(jnp.where(wrap, i + 1, i), jnp.where(wrap, 0, c + 1), 1 - slot):
                cp.start()

        col_a = pl.multiple_of(c * tf, tf)
        col_b = pl.multiple_of(D_FF + c * tf, tf)
        wa = wa_buf[slot]
        wb = wb_buf[slot]
        wd = wd_buf[slot]
        ka = k_ref[:, pl.ds(col_a, tf)]
        kb = k_ref[:, pl.ds(col_b, tf)]
        ea_ref[0:_HALO_SHORT, :] = ca_ref[c]
        eb_ref[0:_HALO_SHORT, :] = cb_ref[c]

        for r0, r1 in blocks:
            hn = hn_ref[r0:r1, :]
            ea_ref[_HALO_SHORT + r0:_HALO_SHORT + r1, :] = _dot(hn, wa)
            eb_ref[_HALO_SHORT + r0:_HALO_SHORT + r1, :] = _dot(hn, wb)

        for r0, r1 in blocks:
            r1p = min(r1, tm)
            act = _silu(conv(ea_ref, ka, r0, r1p)) * conv(eb_ref, kb, r0, r1p)
            if r1 > tm:
                act_s = (_silu(conv_sample(ea_ref, ka, sa_buf[slot]))
                         * conv_sample(eb_ref, kb, sb_buf[slot]))
                res = _dot(jnp.concatenate([act, act_s], axis=0).astype(_BF16), wd)
                o_ref[r0:tm, :] += res[0:tm - r0, :]
                os_ref[...] += res[tm - r0:tm - r0 + ts, :]
            else:
                o_ref[r0:r1, :] += _dot(act.astype(_BF16), wd)

        ca_ref[c] = ea_ref[tm:tm + _HALO_SHORT, :]
        cb_ref[c] = eb_ref[tm:tm + _HALO_SHORT, :]
        tail = slice(tm + _HALO_SHORT - (FFN_K - 1), tm + _HALO_SHORT)
        nf_ref[:, pl.ds(col_a, tf)] = ea_ref[tail, :]
        nf_ref[:, pl.ds(col_b, tf)] = eb_ref[tail, :]
        new = slice(_HALO_SHORT + tm, _HALO_SHORT + tm + ts)
        us_ref[:, pl.ds(col_a, tf)] = ea_ref[new, :]
        us_ref[:, pl.ds(col_b, tf)] = eb_ref[new, :]
        return carry

    lax.fori_loop(0, n_c, chunk, 0)


def _ffn(h, hs, nb, norm_g, w_up, ffn_conv, w_down, state_ffn, layer, *, tm, tf, sub):
    m = h.shape[0]
    n_tiles = m // tm
    n_s = n_tiles // nb
    n_c = D_FF // tf
    ts = hs.shape[0] // n_tiles
    assert ts * n_tiles == hs.shape[0] and ts % _BF16_ROWS == 0, (hs.shape, n_tiles)
    return pl.pallas_call(
        functools.partial(_ffn_kernel, layer=layer, tm=tm, ts=ts, sub=sub, tf=tf, n_c=n_c,
                          n_s=n_s, n_tiles=n_tiles),
        out_shape=(
            jax.ShapeDtypeStruct((m, D_MODEL), _F32),
            jax.ShapeDtypeStruct(hs.shape, _F32),
            jax.ShapeDtypeStruct((n_tiles, FFN_K - 1, 2 * D_FF), _F32),
            jax.ShapeDtypeStruct((hs.shape[0], 2 * D_FF), _F32),
        ),
        grid=(n_tiles,),
        in_specs=[
            _single((tm, D_MODEL), lambda i: (i, 0)),
            pl.BlockSpec((ts, D_MODEL), lambda i: (i, 0)),
            pl.BlockSpec((None, 1, D_MODEL), lambda i: (layer, 0, 0)),
            pl.BlockSpec((None, FFN_K, 2 * D_FF), lambda i: (layer, 0, 0)),
            pl.BlockSpec(memory_space=pl.ANY),
            pl.BlockSpec(memory_space=pl.ANY),
            pl.BlockSpec(memory_space=pl.ANY),
        ],
        out_specs=(
            pl.BlockSpec((tm, D_MODEL), lambda i: (i, 0)),
            pl.BlockSpec((ts, D_MODEL), lambda i: (i, 0)),
            pl.BlockSpec((None, FFN_K - 1, 2 * D_FF), lambda i: (i, 0, 0)),
            pl.BlockSpec((ts, 2 * D_FF), lambda i: (i, 0)),
        ),
        scratch_shapes=[
            pltpu.VMEM((tm + ts, D_MODEL), _BF16),
            pltpu.VMEM((_HALO_SHORT + tm + ts, tf), _F32),
            pltpu.VMEM((_HALO_SHORT + tm + ts, tf), _F32),
            pltpu.VMEM((n_c, _HALO_SHORT, tf), _F32),
            pltpu.VMEM((n_c, _HALO_SHORT, tf), _F32),
            pltpu.VMEM((2, D_MODEL, tf), _BF16),
            pltpu.VMEM((2, D_MODEL, tf), _BF16),
            pltpu.VMEM((2, tf, D_MODEL), _BF16),
            pltpu.VMEM((2, ts, FFN_K - 1, tf), _F32),
            pltpu.VMEM((2, ts, FFN_K - 1, tf), _F32),
            pltpu.SemaphoreType.DMA((5, 2)),
        ],
        compiler_params=_params("arbitrary"),
        name="ffn",
    )(h, hs, _vec3(norm_g), ffn_conv, state_ffn, w_up, w_down)


def _ple_kernel(h_ref, hs_ref, g_ref, gate_ref, p_ref, ps_ref, proj_ref, gf_ref, o_ref, os_ref,
                gate_bf, proj_bf, *, final, tm, sub, n_i):
    i = pl.program_id(0)

    @pl.when(i == 0)
    def _():
        gate_bf[...] = gate_ref[...].astype(_BF16)
        proj_bf[...] = proj_ref[...].astype(_BF16)

    def rows(h, p):
        hn = _rms(h, g_ref[...]).astype(_BF16)
        gate = jax.nn.sigmoid(_dot(hn, gate_bf[...]))
        out = h + _dot(p.astype(_BF16), proj_bf[...]) * gate
        return _rms(out, gf_ref[...]) if final else out

    @pl.when(i < n_i)
    def _():
        for r0 in range(0, tm, sub):
            o_ref[r0:r0 + sub, :] = rows(h_ref[r0:r0 + sub, :], p_ref[r0:r0 + sub, :])

    @pl.when(i == n_i)
    def _():
        os_ref[...] = rows(hs_ref[...], ps_ref[...])


def _ple(h, hs, norm_g, gate_w, p, ps, proj_w, layer, final_g, *, tm, sub, final):
    m = h.shape[0]
    ms = hs.shape[0]
    n_i = m // tm
    row = lambda i: (jnp.minimum(i, n_i - 1), 0)
    return pl.pallas_call(
        functools.partial(_ple_kernel, final=final, tm=tm, sub=sub, n_i=n_i),
        out_shape=(jax.ShapeDtypeStruct((m, D_MODEL), _F32), jax.ShapeDtypeStruct((ms, D_MODEL), _F32)),
        grid=(n_i + 1,),
        in_specs=[
            pl.BlockSpec((tm, D_MODEL), row),
            pl.BlockSpec((ms, D_MODEL), lambda i: (0, 0)),
            pl.BlockSpec((None, 1, D_MODEL), lambda i: (layer, 0, 0)),
            _single((None, D_MODEL, D_MODEL), lambda i: (layer, 0, 0)),
            pl.BlockSpec((None, tm, PLE_DIM), lambda i: (layer, jnp.minimum(i, n_i - 1), 0)),
            pl.BlockSpec((None, ms, PLE_DIM), lambda i: (layer, 0, 0)),
            _single((None, PLE_DIM, D_MODEL), lambda i: (layer, 0, 0)),
            pl.BlockSpec((1, D_MODEL), lambda i: (0, 0)),
        ],
        out_specs=(
            pl.BlockSpec((tm, D_MODEL), row),
            pl.BlockSpec((ms, D_MODEL), lambda i: (0, 0)),
        ),
        scratch_shapes=[
            pltpu.VMEM((D_MODEL, D_MODEL), _BF16),
            pltpu.VMEM((PLE_DIM, D_MODEL), _BF16),
        ],
        compiler_params=_params("arbitrary"),
        name="ple",
    )(h, hs, _vec3(norm_g), gate_w, p, ps, proj_w, final_g.reshape(1, -1))


def kernel(x_prompt, x_sample, p_prompt, p_sample, state_pool, state_conf, state_sc, state_ffn,
           norm_mix, w_in, w_pool, pool_scale, conf_dw, conf_dw_b, conf_ln_g, conf_ln_b,
           conf_pw, conf_pw_b, sc_conv, w_out, norm_ffn, w_up, ffn_conv, w_down,
           norm_ple, ple_gate, ple_proj, norm_final):
    nb, seq, _ = x_prompt.shape
    ns = x_sample.shape[0]
    depth = w_in.shape[0]
    hp = x_prompt.reshape(nb * seq, D_MODEL)
    hs = x_sample.reshape(ns, D_MODEL)
    pp = p_prompt.reshape(depth, nb * seq, PLE_DIM)
    ps = p_sample.reshape(depth, ns, PLE_DIM)
    mix_w = (w_pool, pool_scale, conf_dw, conf_dw_b, conf_ln_g, conf_ln_b, conf_pw, conf_pw_b, sc_conv)

    w_up_bf = w_up.astype(_BF16)
    w_down_bf = w_down.astype(_BF16)

    prompt_states = [[] for _ in range(4)]
    sample_rows = [[] for _ in range(4)]
    for i in range(depth):
        last = i == depth - 1

        z, zs = _norm_matmul(hp, hs, norm_mix, w_in, i, tm=_TM_IN, tn=_TN_IN, sub=_SUB_IN)
        npool_p = z.reshape(nb, seq, IN_COLS)[:, seq - POOL_BUF:, :POOL_WIDTH]
        h1, nconf_p, nsc_p = _mix_out_prompt(z, hp, nb, i, mix_w, w_out, tm=_TM_MIX)
        h1s, g_s, v_s = _mix_out_sample(zs, hs, state_pool, state_conf, state_sc, i, mix_w, w_out)

        h2, h2s, nf, up_s = _ffn(h1, h1s, nb, norm_ffn, w_up_bf, ffn_conv, w_down_bf,
                                 state_ffn, i, tm=_TM_FFN, tf=_TF_FFN, sub=_SUB_FFN)
        tiles_per_seq = nf.shape[0] // nb
        nffn_p = nf[tiles_per_seq - 1::tiles_per_seq]

        hp, hs = _ple(h2, h2s, norm_ple, ple_gate, pp, ps, ple_proj, i, norm_final,
                      tm=_TM_PLE, sub=_SUB_PLE, final=last)

        for lst, val in zip(prompt_states, (npool_p, nconf_p, nsc_p, nffn_p)):
            lst.append(val)
        for lst, val in zip(sample_rows, (zs[:, :POOL_WIDTH], g_s, v_s, up_s)):
            lst.append(val)

    def shifted(state, rows):
        return jnp.concatenate([state[:, :, 1:], jnp.stack(rows)[:, :, None]], axis=2)

    new_p = [jnp.stack(l) for l in prompt_states]
    new_s = [shifted(st, rows) for st, rows in
             zip((state_pool, state_conf, state_sc, state_ffn), sample_rows)]
    y_prompt = hp.reshape(nb, seq, D_MODEL)
    y_sample = hs.reshape(ns, 1, D_MODEL)
    return (y_prompt, y_sample, new_p[0], new_s[0], new_p[1], new_s[1],
            new_p[2], new_s[2], new_p[3], new_s[3])
```

```python
import functools

import jax
import jax.numpy as jnp
from jax import lax
from jax.experimental import pallas as pl
from jax.experimental.pallas import tpu as pltpu

D_MODEL = 2048
POOL_WIDTH = 512
POOL_WINDOWS = (2, 4, 8, 16)
POOL_GW = 128
POOL_BUF = 15
CONV_WIDTH = 768
CONF_K = 31
SC_WIDTH = 768
SC_K = 3
FFN_K = 3
D_FF = 5632
PLE_DIM = 256
IN_COLS = POOL_WIDTH + 2 * CONV_WIDTH + 3 * SC_WIDTH
EPS = 1e-6
PAST_LEN = 16384

_O_U = 0
_O_GA = _O_U + POOL_WIDTH
_O_GB = _O_GA + CONV_WIDTH
_O_SB = _O_GB + CONV_WIDTH
_O_SC = _O_SB + SC_WIDTH
_O_SH = _O_SC + SC_WIDTH
_Y_A = 0
_Y_B = POOL_WIDTH
_Y_C = POOL_WIDTH + CONV_WIDTH

_VMEM_LIMIT = 56 * 1024 * 1024
_LANE = 128
_SUBLANE = 8
_BF16_ROWS = 16

_HALO_POOL = 16
_HALO_CONF = 32
_HALO_SHORT = 8

_TM_IN, _TN_IN, _SUB_IN = 512, IN_COLS // 2, 256
_TM_MIX = 256
_CONV_ROWS = 128
_CONV_BLOCKS_PER_TICK = 3
_OUT_COLS_PER_TICK = 256
_TM_FFN, _TF_FFN, _SUB_FFN = 1024, 512, 512
_TM_PLE, _SUB_PLE = 512, 256
_APPEND_BLOCK_BYTES = 3 * 1024 * 1024

_BF16 = jnp.bfloat16
_F32 = jnp.float32


def _params(*sem):
    return pltpu.CompilerParams(dimension_semantics=sem, vmem_limit_bytes=_VMEM_LIMIT)


def _rms(x, g):
    ms = jnp.mean(x * x, axis=-1, keepdims=True)
    return x * lax.rsqrt(ms + EPS) * g


def _dot(a, b):
    return jnp.dot(a, b, preferred_element_type=_F32)


def _silu(x):
    return x * jax.nn.sigmoid(x)


def _single(block, index_map):
    return pl.BlockSpec(block, index_map, pipeline_mode=pl.Buffered(1))


def _vec3(a):
    return a.reshape(a.shape[0], 1, a.shape[1])


def _norm_matmul_kernel(x_ref, xs_ref, g_ref, w_ref, o_ref, os_ref, wb_ref, *, tm, sub, n_i):
    i = pl.program_id(1)

    @pl.when(i == 0)
    def _():
        wb_ref[...] = w_ref[...].astype(_BF16)

    @pl.when(i < n_i)
    def _():
        for r0 in range(0, tm, sub):
            hn = _rms(x_ref[r0:r0 + sub, :], g_ref[...]).astype(_BF16)
            o_ref[r0:r0 + sub, :] = _dot(hn, wb_ref[...])

    @pl.when(i == n_i)
    def _():
        os_ref[...] = _dot(_rms(xs_ref[...], g_ref[...]).astype(_BF16), wb_ref[...])


def _norm_matmul(x, xs, g, w, layer, *, tm, tn, sub):
    m, k = x.shape
    ms = xs.shape[0]
    n = w.shape[2]
    n_i = m // tm
    row = lambda j, i: (jnp.minimum(i, n_i - 1), 0)
    return pl.pallas_call(
        functools.partial(_norm_matmul_kernel, tm=tm, sub=sub, n_i=n_i),
        out_shape=(jax.ShapeDtypeStruct((m, n), _F32), jax.ShapeDtypeStruct((ms, n), _F32)),
        grid=(n // tn, n_i + 1),
        in_specs=[
            pl.BlockSpec((tm, k), row),
            pl.BlockSpec((ms, k), lambda j, i: (0, 0)),
            pl.BlockSpec((None, 1, k), lambda j, i: (layer, 0, 0)),
            _single((None, k, tn), lambda j, i: (layer, 0, j)),
        ],
        out_specs=(
            pl.BlockSpec((tm, tn), lambda j, i: (jnp.minimum(i, n_i - 1), j)),
            pl.BlockSpec((ms, tn), lambda j, i: (0, j)),
        ),
        scratch_shapes=[pltpu.VMEM((k, tn), _BF16)],
        compiler_params=_params("arbitrary", "arbitrary"),
        name="in_proj",
    )(x, xs, _vec3(g), w)


def _mixer_weight_specs(layer):
    l3 = lambda *_: (layer, 0, 0)
    l4 = lambda *_: (layer, 0, 0, 0)
    return [
        _single((None, len(POOL_WINDOWS), POOL_GW, POOL_GW), l4),
        pl.BlockSpec((None, 1, POOL_WIDTH), l3),
        pl.BlockSpec((None, CONF_K, CONV_WIDTH), l3),
        pl.BlockSpec((None, 1, CONV_WIDTH), l3),
        pl.BlockSpec((None, 1, CONV_WIDTH), l3),
        pl.BlockSpec((None, 1, CONV_WIDTH), l3),
        _single((None, CONV_WIDTH, CONV_WIDTH), l3),
        pl.BlockSpec((None, 1, CONV_WIDTH), l3),
        pl.BlockSpec((None, SC_K, SC_WIDTH), l3),
    ]


def _mixer_weight_args(w_pool, pool_scale, conf_dw, conf_dw_b, conf_ln_g, conf_ln_b,
                       conf_pw, conf_pw_b, sc_conv):
    return (w_pool, _vec3(pool_scale), conf_dw, _vec3(conf_dw_b), _vec3(conf_ln_g),
            _vec3(conf_ln_b), conf_pw, _vec3(conf_pw_b), sc_conv)


def _layernorm_silu(cb, lng_ref, lnb_ref):
    mu = jnp.mean(cb, axis=-1, keepdims=True)
    xc = cb - mu
    var = jnp.mean(xc * xc, axis=-1, keepdims=True)
    return _silu(xc * lax.rsqrt(var + EPS) * lng_ref[...] + lnb_ref[...])


def _conv31_block(eg_ref, dw_ref, c0, t0, rows):
    base = _HALO_CONF - (CONF_K - 1)
    acc = None
    for r in range(_SUBLANE):
        p = None
        for q in range((CONF_K + base) // _SUBLANE + 1):
            k = _SUBLANE * q + r - base
            if 0 <= k < CONF_K:
                lo = t0 + _SUBLANE * q
                term = dw_ref[k:k + 1, c0:c0 + _LANE] * eg_ref[lo:lo + rows + _SUBLANE, c0:c0 + _LANE]
                p = term if p is None else p + term
        shifted = p[r:r + rows, :]
        acc = shifted if acc is None else acc + shifted
    return acc


def _mixers_prompt(z_ref, wpool_ref, pscale_ref, dw_ref, dwb_ref, lng_ref, lnb_ref,
                   pwbf_ref, pwb_ref, scw_ref, y_ref, nconf_ref, nsc_ref,
                   eu_ref, eg_ref, ev_ref, cb_ref, *, tm, s, tick):
    tick()
    ev_ref[_HALO_SHORT:_HALO_SHORT + tm, :] = (
        z_ref[:, _O_SC:_O_SC + SC_WIDTH] * z_ref[:, _O_SH:_O_SH + SC_WIDTH])
    cv = scw_ref[0:1, :] * ev_ref[_HALO_SHORT - 2:_HALO_SHORT - 2 + tm, :]
    cv = cv + scw_ref[1:2, :] * ev_ref[_HALO_SHORT - 1:_HALO_SHORT - 1 + tm, :]
    cv = cv + scw_ref[2:3, :] * ev_ref[_HALO_SHORT:_HALO_SHORT + tm, :]
    y_ref[:, _Y_C:_Y_C + SC_WIDTH] = (z_ref[:, _O_SB:_O_SB + SC_WIDTH] * cv).astype(_BF16)
    nsc_ref[...] = ev_ref[tm + _HALO_SHORT - (SC_K - 1):tm + _HALO_SHORT, :]
    ev_ref[0:_HALO_SHORT, :] = ev_ref[tm:tm + _HALO_SHORT, :]

    eu_ref[_HALO_POOL:_HALO_POOL + tm, :] = z_ref[:, _O_U:_O_U + POOL_WIDTH]
    pos = s * tm + lax.broadcasted_iota(jnp.int32, (tm, 1), 0)
    for g, w in enumerate(POOL_WINDOWS):
        if g % 2 == 0:
            tick()
        c0 = g * POOL_GW
        u = eu_ref[_HALO_POOL:_HALO_POOL + tm, c0:c0 + POOL_GW]
        wsum = u
        for k in range(1, w):
            wsum = wsum + eu_ref[_HALO_POOL - k:_HALO_POOL - k + tm, c0:c0 + POOL_GW]
        inv_cnt = 1.0 / jnp.minimum(w, pos + 1).astype(_F32)
        d = wsum * inv_cnt - u
        y_ref[:, _Y_A + c0:_Y_A + c0 + POOL_GW] = (
            _dot(d, wpool_ref[g]) * pscale_ref[:, c0:c0 + POOL_GW]).astype(_BF16)
    eu_ref[0:_HALO_POOL, :] = eu_ref[tm:tm + _HALO_POOL, :]

    eg_ref[_HALO_CONF:_HALO_CONF + tm, :] = (
        z_ref[:, _O_GA:_O_GA + CONV_WIDTH] * jax.nn.sigmoid(z_ref[:, _O_GB:_O_GB + CONV_WIDTH]))
    blocks = [(t0, c0) for t0 in range(0, tm, _CONV_ROWS) for c0 in range(0, CONV_WIDTH, _LANE)]
    for n, (t0, c0) in enumerate(blocks):
        if n % _CONV_BLOCKS_PER_TICK == 0:
            tick()
        cb_ref[t0:t0 + _CONV_ROWS, c0:c0 + _LANE] = (
            _conv31_block(eg_ref, dw_ref, c0, t0, _CONV_ROWS) + dwb_ref[:, c0:c0 + _LANE])
    tick()
    act = _layernorm_silu(cb_ref[...], lng_ref, lnb_ref).astype(_BF16)
    y_ref[:, _Y_B:_Y_B + CONV_WIDTH] = (_dot(act, pwbf_ref[...]) + pwb_ref[...]).astype(_BF16)
    nconf_ref[...] = eg_ref[tm + _HALO_CONF - (CONF_K - 1):tm + _HALO_CONF, :]
    eg_ref[0:_HALO_CONF, :] = eg_ref[tm:tm + _HALO_CONF, :]


def _mix_out_prompt_kernel(z_ref, h_ref, wpool_ref, pscale_ref, dw_ref, dwb_ref, lng_ref, lnb_ref,
                           pw_ref, pwb_ref, scw_ref, wout_ref,
                           o_ref, nconf_ref, nsc_ref,
                           eu_ref, eg_ref, ev_ref, cb_ref, y0_ref, y1_ref, wo_ref, pwbf_ref,
                           *, tm, n_s, n_tiles):
    i = pl.program_id(0)
    s = jnp.minimum(i, n_tiles - 1) % n_s

    @pl.when(i == 0)
    def _():
        wo_ref[...] = wout_ref[...].astype(_BF16)
        pwbf_ref[...] = pw_ref[...].astype(_BF16)
        eg_ref[_HALO_CONF + tm:_HALO_CONF + tm + _SUBLANE, :] = jnp.zeros((_SUBLANE, CONV_WIDTH), _F32)
        y1_ref[...] = jnp.zeros(y1_ref.shape, _BF16)

    @pl.when(s == 0)
    def _():
        eu_ref[0:_HALO_POOL, :] = jnp.zeros((_HALO_POOL, POOL_WIDTH), _F32)
        eg_ref[0:_HALO_CONF, :] = jnp.zeros((_HALO_CONF, CONV_WIDTH), _F32)
        ev_ref[0:_HALO_SHORT, :] = jnp.zeros((_HALO_SHORT, SC_WIDTH), _F32)

    def step(y_prev_ref, y_cur_ref):
        pending = list(range(0, D_MODEL, _OUT_COLS_PER_TICK))

        def tick():
            if pending:
                c0 = pending.pop(0)
                cols = slice(c0, c0 + _OUT_COLS_PER_TICK)
                o_ref[:, cols] = h_ref[:, cols] + _dot(y_prev_ref[...], wo_ref[:, cols])

        _mixers_prompt(z_ref, wpool_ref, pscale_ref, dw_ref, dwb_ref, lng_ref, lnb_ref,
                       pwbf_ref, pwb_ref, scw_ref, y_cur_ref, nconf_ref, nsc_ref,
                       eu_ref, eg_ref, ev_ref, cb_ref, tm=tm, s=s, tick=tick)
        while pending:
            tick()

    @pl.when(i % 2 == 0)
    def _():
        step(y1_ref, y0_ref)

    @pl.when(i % 2 == 1)
    def _():
        step(y0_ref, y1_ref)


def _mix_out_prompt(z, h, nb, layer, mix_w, w_out, *, tm):
    m = z.shape[0]
    n_tiles = m // tm
    n_s = n_tiles // nb
    cur = lambda i: (jnp.minimum(i, n_tiles - 1), 0)
    prev = lambda i: (jnp.maximum(i - 1, 0), 0)
    seq = lambda i: (jnp.minimum(i, n_tiles - 1) // n_s, 0, 0)
    return pl.pallas_call(
        functools.partial(_mix_out_prompt_kernel, tm=tm, n_s=n_s, n_tiles=n_tiles),
        out_shape=(
            jax.ShapeDtypeStruct((m, D_MODEL), _F32),
            jax.ShapeDtypeStruct((nb, CONF_K - 1, CONV_WIDTH), _F32),
            jax.ShapeDtypeStruct((nb, SC_K - 1, SC_WIDTH), _F32),
        ),
        grid=(n_tiles + 1,),
        in_specs=([pl.BlockSpec((tm, IN_COLS), cur), pl.BlockSpec((tm, D_MODEL), prev)]
                  + _mixer_weight_specs(layer)
                  + [_single((None, D_MODEL, D_MODEL), lambda i: (layer, 0, 0))]),
        out_specs=(
            pl.BlockSpec((tm, D_MODEL), prev),
            pl.BlockSpec((None, CONF_K - 1, CONV_WIDTH), seq),
            pl.BlockSpec((None, SC_K - 1, SC_WIDTH), seq),
        ),
        scratch_shapes=[
            pltpu.VMEM((_HALO_POOL + tm, POOL_WIDTH), _F32),
            pltpu.VMEM((_HALO_CONF + tm + _SUBLANE, CONV_WIDTH), _F32),
            pltpu.VMEM((_HALO_SHORT + tm, SC_WIDTH), _F32),
            pltpu.VMEM((tm, CONV_WIDTH), _F32),
            pltpu.VMEM((tm, D_MODEL), _BF16),
            pltpu.VMEM((tm, D_MODEL), _BF16),
            pltpu.VMEM((D_MODEL, D_MODEL), _BF16),
            pltpu.VMEM((CONV_WIDTH, CONV_WIDTH), _BF16),
        ],
        compiler_params=_params("arbitrary"),
        name="mix_out_prompt",
    )(z, h, *_mixer_weight_args(*mix_w), w_out)


def _mix_out_sample_kernel(z_ref, h_ref, spool_ref, sconf_ref, ssc_ref,
                           wpool_ref, pscale_ref, dw_ref, dwb_ref, lng_ref, lnb_ref,
                           pw_ref, pwb_ref, scw_ref, wout_ref,
                           o_ref, g_ref, v_ref):
    out = h_ref[...]
    for g, w in enumerate(POOL_WINDOWS):
        c0 = g * POOL_GW
        u = z_ref[:, _O_U + c0:_O_U + c0 + POOL_GW]
        wsum = u + jnp.sum(spool_ref[:, POOL_BUF - (w - 1):POOL_BUF, c0:c0 + POOL_GW], axis=1)
        d = wsum / jnp.float32(min(w, PAST_LEN + 1)) - u
        y_a = _dot(d, wpool_ref[g]) * pscale_ref[:, c0:c0 + POOL_GW]
        out = out + _dot(y_a, wout_ref[_Y_A + c0:_Y_A + c0 + POOL_GW, :])

    gl = z_ref[:, _O_GA:_O_GA + CONV_WIDTH] * jax.nn.sigmoid(z_ref[:, _O_GB:_O_GB + CONV_WIDTH])
    g_ref[...] = gl
    acc = dw_ref[CONF_K - 1:CONF_K, :] * gl
    acc = acc + jnp.sum(sconf_ref[...] * dw_ref[0:CONF_K - 1, :][None], axis=1)
    y_b = _dot(_layernorm_silu(acc + dwb_ref[...], lng_ref, lnb_ref), pw_ref[...]) + pwb_ref[...]
    out = out + _dot(y_b, wout_ref[_Y_B:_Y_B + CONV_WIDTH, :])

    v = z_ref[:, _O_SC:_O_SC + SC_WIDTH] * z_ref[:, _O_SH:_O_SH + SC_WIDTH]
    v_ref[...] = v
    cv = jnp.sum(ssc_ref[...] * scw_ref[0:SC_K - 1, :][None], axis=1) + scw_ref[SC_K - 1:SC_K, :] * v
    y_c = z_ref[:, _O_SB:_O_SB + SC_WIDTH] * cv
    o_ref[...] = out + _dot(y_c, wout_ref[_Y_C:_Y_C + SC_WIDTH, :])


def _mix_out_sample(z, h, state_pool, state_conf, state_sc, layer, mix_w, w_out):
    m = z.shape[0]
    full = lambda a: pl.BlockSpec(a.shape, lambda i: (0,) * a.ndim)
    state = lambda a: _single((None,) + a.shape[1:], lambda i: (layer, 0, 0, 0))
    return pl.pallas_call(
        _mix_out_sample_kernel,
        out_shape=(
            jax.ShapeDtypeStruct((m, D_MODEL), _F32),
            jax.ShapeDtypeStruct((m, CONV_WIDTH), _F32),
            jax.ShapeDtypeStruct((m, SC_WIDTH), _F32),
        ),
        grid=(1,),
        in_specs=([full(z), full(h), state(state_pool), state(state_conf), state(state_sc)]
                  + _mixer_weight_specs(layer)
                  + [_single((None, D_MODEL, D_MODEL), lambda i: (layer, 0, 0))]),
        out_specs=(
            pl.BlockSpec((m, D_MODEL), lambda i: (0, 0)),
            pl.BlockSpec((m, CONV_WIDTH), lambda i: (0, 0)),
            pl.BlockSpec((m, SC_WIDTH), lambda i: (0, 0)),
        ),
        compiler_params=_params("arbitrary"),
        name="mix_out_sample",
    )(z, h, state_pool, state_conf, state_sc, *_mixer_weight_args(*mix_w), w_out)


def _ffn_kernel(h_ref, hs_ref, g_ref, k_ref, st_hbm, wup_hbm, wdn_hbm,
                o_ref, os_ref, nf_ref, us_ref,
                hn_ref, ea_ref, eb_ref, ca_ref, cb_ref, wa_buf, wb_buf, wd_buf, sa_buf, sb_buf, sem,
                *, layer, tm, ts, sub, tf, n_c, n_s, n_tiles):
    i = pl.program_id(0)
    s = i % n_s

    def chunk_copies(tile, chunk, slot):
        col_a = pl.multiple_of(chunk * tf, tf)
        col_b = pl.multiple_of(D_FF + chunk * tf, tf)
        rows = pl.ds(pl.multiple_of(tile * ts, ts), ts)
        return (
            pltpu.make_async_copy(wup_hbm.at[layer, :, pl.ds(col_a, tf)], wa_buf.at[slot], sem.at[0, slot]),
            pltpu.make_async_copy(wup_hbm.at[layer, :, pl.ds(col_b, tf)], wb_buf.at[slot], sem.at[1, slot]),
            pltpu.make_async_copy(wdn_hbm.at[layer, pl.ds(col_a, tf), :], wd_buf.at[slot], sem.at[2, slot]),
            pltpu.make_async_copy(st_hbm.at[layer, rows, :, pl.ds(col_a, tf)], sa_buf.at[slot], sem.at[3, slot]),
            pltpu.make_async_copy(st_hbm.at[layer, rows, :, pl.ds(col_b, tf)], sb_buf.at[slot], sem.at[4, slot]),
        )

    @pl.when(i == 0)
    def _():
        for cp in chunk_copies(0, 0, 0):
            cp.start()

    h = h_ref[...]
    hs = hs_ref[...]
    hn_ref[0:tm, :] = _rms(h, g_ref[...]).astype(_BF16)
    hn_ref[tm:tm + ts, :] = _rms(hs, g_ref[...]).astype(_BF16)
    o_ref[...] = h
    os_ref[...] = hs

    @pl.when(s == 0)
    def _():
        ca_ref[...] = jnp.zeros(ca_ref.shape, _F32)
        cb_ref[...] = jnp.zeros(cb_ref.shape, _F32)

    starts = list(range(0, tm, sub))
    blocks = [(r0, r0 + sub) for r0 in starts[:-1]] + [(starts[-1], tm + ts)]

    def conv(e_ref, k, r0, r1):
        lo, n = _HALO_SHORT + r0, r1 - r0
        out = k[0:1, :] * e_ref[lo - 2:lo - 2 + n, :]
        out = out + k[1:2, :] * e_ref[lo - 1:lo - 1 + n, :]
        return out + k[2:3, :] * e_ref[lo:lo + n, :]

    def conv_sample(e_ref, k, st):
        new = e_ref[_HALO_SHORT + tm:_HALO_SHORT + tm + ts, :]
        hist = jnp.sum(st * k[0:FFN_K - 1, :][None], axis=1)
        return hist + k[FFN_K - 1:FFN_K, :] * new

    def chunk(c, carry):
        step = i * n_c + c
        slot = step % 2
        for cp in chunk_copies(i, c, slot):
            cp.wait()

        @pl.when(step + 1 < n_tiles * n_c)
        def _():
            wrap = c + 1 == n_c
            for cp in chunk_copies(jnp.where(wrap, i + 1, i), jnp.where(wrap, 0, c + 1), 1 - slot):
                cp.start()

        col_a = pl.multiple_of(c * tf, tf)
        col_b = pl.multiple_of(D_FF + c * tf, tf)
        wa = wa_buf[slot]
        wb = wb_buf[slot]
        wd = wd_buf[slot]
        ka = k_ref[:, pl.ds(col_a, tf)]
        kb = k_ref[:, pl.ds(col_b, tf)]
        ea_ref[0:_HALO_SHORT, :] = ca_ref[c]
        eb_ref[0:_HALO_SHORT, :] = cb_ref[c]

        for r0, r1 in blocks:
            hn = hn_ref[r0:r1, :]
            ea_ref[_HALO_SHORT + r0:_HALO_SHORT + r1, :] = _dot(hn, wa)
            eb_ref[_HALO_SHORT + r0:_HALO_SHORT + r1, :] = _dot(hn, wb)

        for r0, r1 in blocks:
            r1p = min(r1, tm)
            act = _silu(conv(ea_ref, ka, r0, r1p)) * conv(eb_ref, kb, r0, r1p)
            if r1 > tm:
                act_s = (_silu(conv_sample(ea_ref, ka, sa_buf[slot]))
                         * conv_sample(eb_ref, kb, sb_buf[slot]))
                res = _dot(jnp.concatenate([act, act_s], axis=0).astype(_BF16), wd)
                o_ref[r0:tm, :] += res[0:tm - r0, :]
                os_ref[...] += res[tm - r0:tm - r0 + ts, :]
            else:
                o_ref[r0:r1, :] += _dot(act.astype(_BF16), wd)

        ca_ref[c] = ea_ref[tm:tm + _HALO_SHORT, :]
        cb_ref[c] = eb_ref[tm:tm + _HALO_SHORT, :]
        tail = slice(tm + _HALO_SHORT - (FFN_K - 1), tm + _HALO_SHORT)
        nf_ref[:, pl.ds(col_a, tf)] = ea_ref[tail, :]
        nf_ref[:, pl.ds(col_b, tf)] = eb_ref[tail, :]
        new = slice(_HALO_SHORT + tm, _HALO_SHORT + tm + ts)
        us_ref[:, pl.ds(col_a, tf)] = ea_ref[new, :]
        us_ref[:, pl.ds(col_b, tf)] = eb_ref[new, :]
        return carry

    lax.fori_loop(0, n_c, chunk, 0)


def _ffn(h, hs, nb, norm_g, w_up, ffn_conv, w_down, state_ffn, layer, *, tm, tf, sub):
    m = h.shape[0]
    n_tiles = m // tm
    n_s = n_tiles // nb
    n_c = D_FF // tf
    ts = hs.shape[0] // n_tiles
    assert ts * n_tiles == hs.shape[0] and ts % _BF16_ROWS == 0, (hs.shape, n_tiles)
    return pl.pallas_call(
        functools.partial(_ffn_kernel, layer=layer, tm=tm, ts=ts, sub=sub, tf=tf, n_c=n_c,
                          n_s=n_s, n_tiles=n_tiles),
        out_shape=(
            jax.ShapeDtypeStruct((m, D_MODEL), _F32),
            jax.ShapeDtypeStruct(hs.shape, _F32),
            jax.ShapeDtypeStruct((n_tiles, FFN_K - 1, 2 * D_FF), _F32),
            jax.ShapeDtypeStruct((hs.shape[0], 2 * D_FF), _F32),
        ),
        grid=(n_tiles,),
        in_specs=[
            _single((tm, D_MODEL), lambda i: (i, 0)),
            pl.BlockSpec((ts, D_MODEL), lambda i: (i, 0)),
            pl.BlockSpec((None, 1, D_MODEL), lambda i: (layer, 0, 0)),
            pl.BlockSpec((None, FFN_K, 2 * D_FF), lambda i: (layer, 0, 0)),
            pl.BlockSpec(memory_space=pl.ANY),
            pl.BlockSpec(memory_space=pl.ANY),
            pl.BlockSpec(memory_space=pl.ANY),
        ],
        out_specs=(
            pl.BlockSpec((tm, D_MODEL), lambda i: (i, 0)),
            pl.BlockSpec((ts, D_MODEL), lambda i: (i, 0)),
            pl.BlockSpec((None, FFN_K - 1, 2 * D_FF), lambda i: (i, 0, 0)),
            pl.BlockSpec((ts, 2 * D_FF), lambda i: (i, 0)),
        ),
        scratch_shapes=[
            pltpu.VMEM((tm + ts, D_MODEL), _BF16),
            pltpu.VMEM((_HALO_SHORT + tm + ts, tf), _F32),
            pltpu.VMEM((_HALO_SHORT + tm + ts, tf), _F32),
            pltpu.VMEM((n_c, _HALO_SHORT, tf), _F32),
            pltpu.VMEM((n_c, _HALO_SHORT, tf), _F32),
            pltpu.VMEM((2, D_MODEL, tf), _BF16),
            pltpu.VMEM((2, D_MODEL, tf), _BF16),
            pltpu.VMEM((2, tf, D_MODEL), _BF16),
            pltpu.VMEM((2, ts, FFN_K - 1, tf), _F32),
            pltpu.VMEM((2, ts, FFN_K - 1, tf), _F32),
            pltpu.SemaphoreType.DMA((5, 2)),
        ],
        compiler_params=_params("arbitrary"),
        name="ffn",
    )(h, hs, _vec3(norm_g), ffn_conv, state_ffn, w_up, w_down)


def _ple_kernel(h_ref, hs_ref, g_ref, gate_ref, p_ref, ps_ref, proj_ref, gf_ref, o_ref, os_ref,
                gate_bf, proj_bf, *, final, tm, sub, n_i):
    i = pl.program_id(0)

    @pl.when(i == 0)
    def _():
        gate_bf[...] = gate_ref[...].astype(_BF16)
        proj_bf[...] = proj_ref[...].astype(_BF16)

    def rows(h, p):
        hn = _rms(h, g_ref[...]).astype(_BF16)
        gate = jax.nn.sigmoid(_dot(hn, gate_bf[...]))
        out = h + _dot(p.astype(_BF16), proj_bf[...]) * gate
        return _rms(out, gf_ref[...]) if final else out

    @pl.when(i < n_i)
    def _():
        for r0 in range(0, tm, sub):
            o_ref[r0:r0 + sub, :] = rows(h_ref[r0:r0 + sub, :], p_ref[r0:r0 + sub, :])

    @pl.when(i == n_i)
    def _():
        os_ref[...] = rows(hs_ref[...], ps_ref[...])


def _ple(h, hs, norm_g, gate_w, p, ps, proj_w, layer, final_g, *, tm, sub, final):
    m = h.shape[0]
    ms = hs.shape[0]
    n_i = m // tm
    row = lambda i: (jnp.minimum(i, n_i - 1), 0)
    return pl.pallas_call(
        functools.partial(_ple_kernel, final=final, tm=tm, sub=sub, n_i=n_i),
        out_shape=(jax.ShapeDtypeStruct((m, D_MODEL), _F32), jax.ShapeDtypeStruct((ms, D_MODEL), _F32)),
        grid=(n_i + 1,),
        in_specs=[
            pl.BlockSpec((tm, D_MODEL), row),
            pl.BlockSpec((ms, D_MODEL), lambda i: (0, 0)),
            pl.BlockSpec((None, 1, D_MODEL), lambda i: (layer, 0, 0)),
            _single((None, D_MODEL, D_MODEL), lambda i: (layer, 0, 0)),
            pl.BlockSpec((None, tm, PLE_DIM), lambda i: (layer, jnp.minimum(i, n_i - 1), 0)),
            pl.BlockSpec((None, ms, PLE_DIM), lambda i: (layer, 0, 0)),
            _single((None, PLE_DIM, D_MODEL), lambda i: (layer, 0, 0)),
            pl.BlockSpec((1, D_MODEL), lambda i: (0, 0)),
        ],
        out_specs=(
            pl.BlockSpec((tm, D_MODEL), row),
            pl.BlockSpec((ms, D_MODEL), lambda i: (0, 0)),
        ),
        scratch_shapes=[
            pltpu.VMEM((D_MODEL, D_MODEL), _BF16),
            pltpu.VMEM((PLE_DIM, D_MODEL), _BF16),
        ],
        compiler_params=_params("arbitrary"),
        name="ple",
    )(h, hs, _vec3(norm_g), gate_w, p, ps, proj_w, final_g.reshape(1, -1))


def _append_row_kernel(s_ref, n_ref, o_ref):
    k = s_ref.shape[1]
    o_ref[:, 0:k - 1, :] = s_ref[:, 1:k, :]
    o_ref[:, k - 1:k, :] = n_ref[...][:, None, :]


def _append_row(state, rows):
    depth, b, k, c = state.shape
    slab = -(-k // _SUBLANE) * _SUBLANE * c * state.dtype.itemsize
    bb = b
    while bb > 1 and bb * slab > _APPEND_BLOCK_BYTES:
        bb //= 2
    assert b % bb == 0
    return pl.pallas_call(
        _append_row_kernel,
        out_shape=jax.ShapeDtypeStruct(state.shape, state.dtype),
        grid=(depth, b // bb),
        in_specs=[
            pl.BlockSpec((None, bb, k, c), lambda l, j: (l, j, 0, 0)),
            pl.BlockSpec((None, bb, c), lambda l, j: (l, j, 0)),
        ],
        out_specs=pl.BlockSpec((None, bb, k, c), lambda l, j: (l, j, 0, 0)),
        compiler_params=_params("arbitrary", "arbitrary"),
        name="append_row",
    )(state, rows)


def kernel(x_prompt, x_sample, p_prompt, p_sample, state_pool, state_conf, state_sc, state_ffn,
           norm_mix, w_in, w_pool, pool_scale, conf_dw, conf_dw_b, conf_ln_g, conf_ln_b,
           conf_pw, conf_pw_b, sc_conv, w_out, norm_ffn, w_up, ffn_conv, w_down,
           norm_ple, ple_gate, ple_proj, norm_final):
    nb, seq, _ = x_prompt.shape
    ns = x_sample.shape[0]
    depth = w_in.shape[0]
    hp = x_prompt.reshape(nb * seq, D_MODEL)
    hs = x_sample.reshape(ns, D_MODEL)
    pp = p_prompt.reshape(depth, nb * seq, PLE_DIM)
    ps = p_sample.reshape(depth, ns, PLE_DIM)
    mix_w = (w_pool, pool_scale, conf_dw, conf_dw_b, conf_ln_g, conf_ln_b, conf_pw, conf_pw_b, sc_conv)

    w_up_bf = w_up.astype(_BF16)
    w_down_bf = w_down.astype(_BF16)

    prompt_states = [[] for _ in range(4)]
    sample_rows = [[] for _ in range(4)]
    for i in range(depth):
        last = i == depth - 1

        z, zs = _norm_matmul(hp, hs, norm_mix, w_in, i, tm=_TM_IN, tn=_TN_IN, sub=_SUB_IN)
        npool_p = z.reshape(nb, seq, IN_COLS)[:, seq - POOL_BUF:, :POOL_WIDTH]
        h1, nconf_p, nsc_p = _mix_out_prompt(z, hp, nb, i, mix_w, w_out, tm=_TM_MIX)
        h1s, g_s, v_s = _mix_out_sample(zs, hs, state_pool, state_conf, state_sc, i, mix_w, w_out)

        h2, h2s, nf, up_s = _ffn(h1, h1s, nb, norm_ffn, w_up_bf, ffn_conv, w_down_bf,
                                 state_ffn, i, tm=_TM_FFN, tf=_TF_FFN, sub=_SUB_FFN)
        tiles_per_seq = nf.shape[0] // nb
        nffn_p = nf[tiles_per_seq - 1::tiles_per_seq]

        hp, hs = _ple(h2, h2s, norm_ple, ple_gate, pp, ps, ple_proj, i, norm_final,
                      tm=_TM_PLE, sub=_SUB_PLE, final=last)

        for lst, val in zip(prompt_states, (npool_p, nconf_p, nsc_p, nffn_p)):
            lst.append(val)
        for lst, val in zip(sample_rows, (zs[:, :POOL_WIDTH], g_s, v_s, up_s)):
            lst.append(val)

    new_p = [jnp.stack(l) for l in prompt_states]
    new_s = [_append_row(st, jnp.stack(rows)) for st, rows in
             zip((state_pool, state_conf, state_sc, state_ffn), sample_rows)]
    y_prompt = hp.reshape(nb, seq, D_MODEL)
    y_sample = hs.reshape(ns, 1, D_MODEL)
    return (y_prompt, y_sample, new_p[0], new_s[0], new_p[1], new_s[1],
            new_p[2], new_s[2], new_p[3], new_s[3])
```

```python
import functools

import jax
import jax.numpy as jnp
from jax import lax
from jax.experimental import pallas as pl
from jax.experimental.pallas import tpu as pltpu

D_MODEL = 2048
POOL_WIDTH = 512
POOL_WINDOWS = (2, 4, 8, 16)
POOL_GW = 128
POOL_BUF = 15
CONV_WIDTH = 768
CONF_K = 31
SC_WIDTH = 768
SC_K = 3
FFN_K = 3
D_FF = 5632
PLE_DIM = 256
IN_COLS = POOL_WIDTH + 2 * CONV_WIDTH + 3 * SC_WIDTH
EPS = 1e-6
PAST_LEN = 16384

_O_U = 0
_O_GA = _O_U + POOL_WIDTH
_O_GB = _O_GA + CONV_WIDTH
_O_SB = _O_GB + CONV_WIDTH
_O_SC = _O_SB + SC_WIDTH
_O_SH = _O_SC + SC_WIDTH
_Y_A = 0
_Y_B = POOL_WIDTH
_Y_C = POOL_WIDTH + CONV_WIDTH

_VMEM_LIMIT = 56 * 1024 * 1024
_LANE = 128
_SUBLANE = 8
_BF16_ROWS = 16

_HALO_POOL = 16
_HALO_CONF = 32
_HALO_SHORT = 8

_TM_IN, _TN_IN, _SUB_IN = 512, IN_COLS // 2, 256
_STAGE_ROWS = 256
_TM_MIX = 256
_CONV_ROWS = 128
_CONV_BLOCKS_PER_TICK = 3
_OUT_COLS_PER_TICK = 256
_TM_FFN, _TF_FFN, _SUB_FFN = 1024, 512, 512
_TM_PLE, _SUB_PLE = 512, 256

_BF16 = jnp.bfloat16
_F32 = jnp.float32


def _params(*sem):
    return pltpu.CompilerParams(dimension_semantics=sem, vmem_limit_bytes=_VMEM_LIMIT)


def _rms(x, g):
    ms = jnp.mean(x * x, axis=-1, keepdims=True)
    return x * lax.rsqrt(ms + EPS) * g


def _dot(a, b):
    return jnp.dot(a, b, preferred_element_type=_F32)


def _silu(x):
    return x * jax.nn.sigmoid(x)


def _single(block, index_map):
    return pl.BlockSpec(block, index_map, pipeline_mode=pl.Buffered(1))


def _vec3(a):
    return a.reshape(a.shape[0], 1, a.shape[1])


def _norm_matmul_kernel(x_ref, xs_ref, g_ref, w_hbm, wu_ref, wd_ref,
                        o_ref, os_ref, wub_ref, wdb_ref,
                        wb_ref, stage_ref, sem, *, layer, tm, tn, sub, n_i):
    j = pl.program_id(0)
    i = pl.program_id(1)

    @pl.when(i == 0)
    def _():
        rows = stage_ref.shape[1]
        col = pl.multiple_of(j * tn, _LANE)

        def piece(r, slot):
            return pltpu.make_async_copy(w_hbm.at[layer, pl.ds(r * rows, rows), pl.ds(col, tn)],
                                         stage_ref.at[slot], sem.at[slot])

        n = wb_ref.shape[0] // rows
        piece(0, 0).start()
        for r in range(n):
            piece(r, r % 2).wait()
            if r + 1 < n:
                piece(r + 1, (r + 1) % 2).start()
            wb_ref[r * rows:(r + 1) * rows, :] = stage_ref[r % 2].astype(_BF16)

    @pl.when(i < n_i)
    def _():
        for r0 in range(0, tm, sub):
            hn = _rms(x_ref[r0:r0 + sub, :], g_ref[...]).astype(_BF16)
            o_ref[r0:r0 + sub, :] = _dot(hn, wb_ref[...])
        wub_ref[...] = wu_ref[...].astype(_BF16)
        wdb_ref[...] = wd_ref[...].astype(_BF16)

    @pl.when(i == n_i)
    def _():
        os_ref[...] = _dot(_rms(xs_ref[...], g_ref[...]).astype(_BF16), wb_ref[...])


def _norm_matmul(x, xs, g, w, w_up, w_down, layer, *, tm, tn, sub):
    m, k = x.shape
    ms = xs.shape[0]
    n = w.shape[2]
    n_i = m // tm
    n_j = n // tn
    n_cast = n_j * n_i
    ru, rd = w_up.shape[1] // n_cast, w_down.shape[1] // n_cast
    assert ru * n_cast == w_up.shape[1] and rd * n_cast == w_down.shape[1]
    assert ru % _BF16_ROWS == 0 and rd % _BF16_ROWS == 0 and k % _STAGE_ROWS == 0
    row = lambda j, i: (jnp.minimum(i, n_i - 1), 0)
    cast_in = lambda j, i: (layer, j * n_i + jnp.minimum(i, n_i - 1), 0)
    cast_out = lambda j, i: (j * n_i + jnp.minimum(i, n_i - 1), 0)
    return pl.pallas_call(
        functools.partial(_norm_matmul_kernel, layer=layer, tm=tm, tn=tn, sub=sub, n_i=n_i),
        out_shape=(
            jax.ShapeDtypeStruct((m, n), _F32),
            jax.ShapeDtypeStruct((ms, n), _F32),
            jax.ShapeDtypeStruct(w_up.shape[1:], _BF16),
            jax.ShapeDtypeStruct(w_down.shape[1:], _BF16),
        ),
        grid=(n_j, n_i + 1),
        in_specs=[
            pl.BlockSpec((tm, k), row),
            pl.BlockSpec((ms, k), lambda j, i: (0, 0)),
            pl.BlockSpec((None, 1, k), lambda j, i: (layer, 0, 0)),
            pl.BlockSpec(memory_space=pl.ANY),
            pl.BlockSpec((None, ru, w_up.shape[2]), cast_in),
            pl.BlockSpec((None, rd, w_down.shape[2]), cast_in),
        ],
        out_specs=(
            pl.BlockSpec((tm, tn), lambda j, i: (jnp.minimum(i, n_i - 1), j)),
            pl.BlockSpec((ms, tn), lambda j, i: (0, j)),
            pl.BlockSpec((ru, w_up.shape[2]), cast_out),
            pl.BlockSpec((rd, w_down.shape[2]), cast_out),
        ),
        scratch_shapes=[
            pltpu.VMEM((k, tn), _BF16),
            pltpu.VMEM((2, _STAGE_ROWS, tn), _F32),
            pltpu.SemaphoreType.DMA((2,)),
        ],
        compiler_params=_params("arbitrary", "arbitrary"),
        name="in_proj",
    )(x, xs, _vec3(g), w, w_up, w_down)


def _mixer_weight_specs(layer):
    l3 = lambda *_: (layer, 0, 0)
    l4 = lambda *_: (layer, 0, 0, 0)
    return [
        _single((None, len(POOL_WINDOWS), POOL_GW, POOL_GW), l4),
        pl.BlockSpec((None, 1, POOL_WIDTH), l3),
        pl.BlockSpec((None, CONF_K, CONV_WIDTH), l3),
        pl.BlockSpec((None, 1, CONV_WIDTH), l3),
        pl.BlockSpec((None, 1, CONV_WIDTH), l3),
        pl.BlockSpec((None, 1, CONV_WIDTH), l3),
        _single((None, CONV_WIDTH, CONV_WIDTH), l3),
        pl.BlockSpec((None, 1, CONV_WIDTH), l3),
        pl.BlockSpec((None, SC_K, SC_WIDTH), l3),
    ]


def _mixer_weight_args(w_pool, pool_scale, conf_dw, conf_dw_b, conf_ln_g, conf_ln_b,
                       conf_pw, conf_pw_b, sc_conv):
    return (w_pool, _vec3(pool_scale), conf_dw, _vec3(conf_dw_b), _vec3(conf_ln_g),
            _vec3(conf_ln_b), conf_pw, _vec3(conf_pw_b), sc_conv)


def _layernorm_silu(cb, lng_ref, lnb_ref):
    mu = jnp.mean(cb, axis=-1, keepdims=True)
    xc = cb - mu
    var = jnp.mean(xc * xc, axis=-1, keepdims=True)
    return _silu(xc * lax.rsqrt(var + EPS) * lng_ref[...] + lnb_ref[...])


def _conv31_block(eg_ref, dw_ref, c0, t0, rows):
    base = _HALO_CONF - (CONF_K - 1)
    acc = None
    for r in range(_SUBLANE):
        p = None
        for q in range((CONF_K + base) // _SUBLANE + 1):
            k = _SUBLANE * q + r - base
            if 0 <= k < CONF_K:
                lo = t0 + _SUBLANE * q
                term = dw_ref[k:k + 1, c0:c0 + _LANE] * eg_ref[lo:lo + rows + _SUBLANE, c0:c0 + _LANE]
                p = term if p is None else p + term
        shifted = p[r:r + rows, :]
        acc = shifted if acc is None else acc + shifted
    return acc


def _mixers_prompt(z_ref, wpool_ref, pscale_ref, dw_ref, dwb_ref, lng_ref, lnb_ref,
                   pwbf_ref, pwb_ref, scw_ref, y_ref, nconf_ref, nsc_ref,
                   eu_ref, eg_ref, ev_ref, cb_ref, *, tm, s, tick):
    tick()
    ev_ref[_HALO_SHORT:_HALO_SHORT + tm, :] = (
        z_ref[:, _O_SC:_O_SC + SC_WIDTH] * z_ref[:, _O_SH:_O_SH + SC_WIDTH])
    cv = scw_ref[0:1, :] * ev_ref[_HALO_SHORT - 2:_HALO_SHORT - 2 + tm, :]
    cv = cv + scw_ref[1:2, :] * ev_ref[_HALO_SHORT - 1:_HALO_SHORT - 1 + tm, :]
    cv = cv + scw_ref[2:3, :] * ev_ref[_HALO_SHORT:_HALO_SHORT + tm, :]
    y_ref[:, _Y_C:_Y_C + SC_WIDTH] = (z_ref[:, _O_SB:_O_SB + SC_WIDTH] * cv).astype(_BF16)
    nsc_ref[...] = ev_ref[tm + _HALO_SHORT - (SC_K - 1):tm + _HALO_SHORT, :]
    ev_ref[0:_HALO_SHORT, :] = ev_ref[tm:tm + _HALO_SHORT, :]

    eu_ref[_HALO_POOL:_HALO_POOL + tm, :] = z_ref[:, _O_U:_O_U + POOL_WIDTH]
    pos = s * tm + lax.broadcasted_iota(jnp.int32, (tm, 1), 0)
    for g, w in enumerate(POOL_WINDOWS):
        if g % 2 == 0:
            tick()
        c0 = g * POOL_GW
        u = eu_ref[_HALO_POOL:_HALO_POOL + tm, c0:c0 + POOL_GW]
        wsum = u
        for k in range(1, w):
            wsum = wsum + eu_ref[_HALO_POOL - k:_HALO_POOL - k + tm, c0:c0 + POOL_GW]
        cnt = jnp.minimum(w, pos + 1).astype(_F32)
        d = wsum / cnt - u
        y_ref[:, _Y_A + c0:_Y_A + c0 + POOL_GW] = (
            _dot(d, wpool_ref[g]) * pscale_ref[:, c0:c0 + POOL_GW]).astype(_BF16)
    eu_ref[0:_HALO_POOL, :] = eu_ref[tm:tm + _HALO_POOL, :]

    eg_ref[_HALO_CONF:_HALO_CONF + tm, :] = (
        z_ref[:, _O_GA:_O_GA + CONV_WIDTH] * jax.nn.sigmoid(z_ref[:, _O_GB:_O_GB + CONV_WIDTH]))
    blocks = [(t0, c0) for t0 in range(0, tm, _CONV_ROWS) for c0 in range(0, CONV_WIDTH, _LANE)]
    for n, (t0, c0) in enumerate(blocks):
        if n % _CONV_BLOCKS_PER_TICK == 0:
            tick()
        cb_ref[t0:t0 + _CONV_ROWS, c0:c0 + _LANE] = (
            _conv31_block(eg_ref, dw_ref, c0, t0, _CONV_ROWS) + dwb_ref[:, c0:c0 + _LANE])
    tick()
    act = _layernorm_silu(cb_ref[...], lng_ref, lnb_ref).astype(_BF16)
    y_ref[:, _Y_B:_Y_B + CONV_WIDTH] = (_dot(act, pwbf_ref[...]) + pwb_ref[...]).astype(_BF16)
    nconf_ref[...] = eg_ref[tm + _HALO_CONF - (CONF_K - 1):tm + _HALO_CONF, :]
    eg_ref[0:_HALO_CONF, :] = eg_ref[tm:tm + _HALO_CONF, :]


def _mix_out_prompt_kernel(z_ref, h_ref, wpool_ref, pscale_ref, dw_ref, dwb_ref, lng_ref, lnb_ref,
                           pw_ref, pwb_ref, scw_ref, wout_ref,
                           o_ref, nconf_ref, nsc_ref,
                           eu_ref, eg_ref, ev_ref, cb_ref, y0_ref, y1_ref, wo_ref, pwbf_ref,
                           *, tm, n_s, n_tiles):
    i = pl.program_id(0)
    s = jnp.minimum(i, n_tiles - 1) % n_s

    @pl.when(i == 0)
    def _():
        wo_ref[...] = wout_ref[...].astype(_BF16)
        pwbf_ref[...] = pw_ref[...].astype(_BF16)
        eg_ref[_HALO_CONF + tm:_HALO_CONF + tm + _SUBLANE, :] = jnp.zeros((_SUBLANE, CONV_WIDTH), _F32)
        y1_ref[...] = jnp.zeros(y1_ref.shape, _BF16)

    @pl.when(s == 0)
    def _():
        eu_ref[0:_HALO_POOL, :] = jnp.zeros((_HALO_POOL, POOL_WIDTH), _F32)
        eg_ref[0:_HALO_CONF, :] = jnp.zeros((_HALO_CONF, CONV_WIDTH), _F32)
        ev_ref[0:_HALO_SHORT, :] = jnp.zeros((_HALO_SHORT, SC_WIDTH), _F32)

    def step(y_prev_ref, y_cur_ref):
        pending = list(range(0, D_MODEL, _OUT_COLS_PER_TICK))

        def tick():
            if pending:
                c0 = pending.pop(0)
                cols = slice(c0, c0 + _OUT_COLS_PER_TICK)
                o_ref[:, cols] = h_ref[:, cols] + _dot(y_prev_ref[...], wo_ref[:, cols])

        _mixers_prompt(z_ref, wpool_ref, pscale_ref, dw_ref, dwb_ref, lng_ref, lnb_ref,
                       pwbf_ref, pwb_ref, scw_ref, y_cur_ref, nconf_ref, nsc_ref,
                       eu_ref, eg_ref, ev_ref, cb_ref, tm=tm, s=s, tick=tick)
        while pending:
            tick()

    @pl.when(i % 2 == 0)
    def _():
        step(y1_ref, y0_ref)

    @pl.when(i % 2 == 1)
    def _():
        step(y0_ref, y1_ref)


def _mix_out_prompt(z, h, nb, layer, mix_w, w_out, *, tm):
    m = z.shape[0]
    n_tiles = m // tm
    n_s = n_tiles // nb
    cur = lambda i: (jnp.minimum(i, n_tiles - 1), 0)
    prev = lambda i: (jnp.maximum(i - 1, 0), 0)
    seq = lambda i: (jnp.minimum(i, n_tiles - 1) // n_s, 0, 0)
    return pl.pallas_call(
        functools.partial(_mix_out_prompt_kernel, tm=tm, n_s=n_s, n_tiles=n_tiles),
        out_shape=(
            jax.ShapeDtypeStruct((m, D_MODEL), _F32),
            jax.ShapeDtypeStruct((nb, CONF_K - 1, CONV_WIDTH), _F32),
            jax.ShapeDtypeStruct((nb, SC_K - 1, SC_WIDTH), _F32),
        ),
        grid=(n_tiles + 1,),
        in_specs=([pl.BlockSpec((tm, IN_COLS), cur), pl.BlockSpec((tm, D_MODEL), prev)]
                  + _mixer_weight_specs(layer)
                  + [_single((None, D_MODEL, D_MODEL), lambda i: (layer, 0, 0))]),
        out_specs=(
            pl.BlockSpec((tm, D_MODEL), prev),
            pl.BlockSpec((None, CONF_K - 1, CONV_WIDTH), seq),
            pl.BlockSpec((None, SC_K - 1, SC_WIDTH), seq),
        ),
        scratch_shapes=[
            pltpu.VMEM((_HALO_POOL + tm, POOL_WIDTH), _F32),
            pltpu.VMEM((_HALO_CONF + tm + _SUBLANE, CONV_WIDTH), _F32),
            pltpu.VMEM((_HALO_SHORT + tm, SC_WIDTH), _F32),
            pltpu.VMEM((tm, CONV_WIDTH), _F32),
            pltpu.VMEM((tm, D_MODEL), _BF16),
            pltpu.VMEM((tm, D_MODEL), _BF16),
            pltpu.VMEM((D_MODEL, D_MODEL), _BF16),
            pltpu.VMEM((CONV_WIDTH, CONV_WIDTH), _BF16),
        ],
        compiler_params=_params("arbitrary"),
        name="mix_out_prompt",
    )(z, h, *_mixer_weight_args(*mix_w), w_out)


def _mix_out_sample_kernel(z_ref, h_ref, spool_ref, sconf_ref, ssc_ref,
                           wpool_ref, pscale_ref, dw_ref, dwb_ref, lng_ref, lnb_ref,
                           pw_ref, pwb_ref, scw_ref, wout_ref,
                           o_ref, g_ref, v_ref):
    out = h_ref[...]
    for g, w in enumerate(POOL_WINDOWS):
        c0 = g * POOL_GW
        u = z_ref[:, _O_U + c0:_O_U + c0 + POOL_GW]
        wsum = u + jnp.sum(spool_ref[:, POOL_BUF - (w - 1):POOL_BUF, c0:c0 + POOL_GW], axis=1)
        d = wsum / jnp.float32(min(w, PAST_LEN + 1)) - u
        y_a = _dot(d, wpool_ref[g]) * pscale_ref[:, c0:c0 + POOL_GW]
        out = out + _dot(y_a, wout_ref[_Y_A + c0:_Y_A + c0 + POOL_GW, :])

    gl = z_ref[:, _O_GA:_O_GA + CONV_WIDTH] * jax.nn.sigmoid(z_ref[:, _O_GB:_O_GB + CONV_WIDTH])
    g_ref[...] = gl
    acc = dw_ref[CONF_K - 1:CONF_K, :] * gl
    acc = acc + jnp.sum(sconf_ref[...] * dw_ref[0:CONF_K - 1, :][None], axis=1)
    y_b = _dot(_layernorm_silu(acc + dwb_ref[...], lng_ref, lnb_ref), pw_ref[...]) + pwb_ref[...]
    out = out + _dot(y_b, wout_ref[_Y_B:_Y_B + CONV_WIDTH, :])

    v = z_ref[:, _O_SC:_O_SC + SC_WIDTH] * z_ref[:, _O_SH:_O_SH + SC_WIDTH]
    v_ref[...] = v
    cv = jnp.sum(ssc_ref[...] * scw_ref[0:SC_K - 1, :][None], axis=1) + scw_ref[SC_K - 1:SC_K, :] * v
    y_c = z_ref[:, _O_SB:_O_SB + SC_WIDTH] * cv
    o_ref[...] = out + _dot(y_c, wout_ref[_Y_C:_Y_C + SC_WIDTH, :])


def _mix_out_sample(z, h, state_pool, state_conf, state_sc, layer, mix_w, w_out):
    m = z.shape[0]
    full = lambda a: pl.BlockSpec(a.shape, lambda i: (0,) * a.ndim)
    state = lambda a: _single((None,) + a.shape[1:], lambda i: (layer, 0, 0, 0))
    return pl.pallas_call(
        _mix_out_sample_kernel,
        out_shape=(
            jax.ShapeDtypeStruct((m, D_MODEL), _F32),
            jax.ShapeDtypeStruct((m, CONV_WIDTH), _F32),
            jax.ShapeDtypeStruct((m, SC_WIDTH), _F32),
        ),
        grid=(1,),
        in_specs=([full(z), full(h), state(state_pool), state(state_conf), state(state_sc)]
                  + _mixer_weight_specs(layer)
                  + [_single((None, D_MODEL, D_MODEL), lambda i: (layer, 0, 0))]),
        out_specs=(
            pl.BlockSpec((m, D_MODEL), lambda i: (0, 0)),
            pl.BlockSpec((m, CONV_WIDTH), lambda i: (0, 0)),
            pl.BlockSpec((m, SC_WIDTH), lambda i: (0, 0)),
        ),
        compiler_params=_params("arbitrary"),
        name="mix_out_sample",
    )(z, h, state_pool, state_conf, state_sc, *_mixer_weight_args(*mix_w), w_out)


def _ffn_kernel(h_ref, hs_ref, g_ref, k_ref, st_hbm, wup_hbm, wdn_hbm,
                o_ref, os_ref, nf_ref, us_ref,
                hn_ref, ea_ref, eb_ref, ca_ref, cb_ref, wa_buf, wb_buf, wd_buf, sa_buf, sb_buf, sem,
                *, layer, tm, ts, sub, tf, n_c, n_s, n_tiles):
    i = pl.program_id(0)
    s = i % n_s

    def chunk_copies(tile, chunk, slot):
        col_a = pl.multiple_of(chunk * tf, tf)
        col_b = pl.multiple_of(D_FF + chunk * tf, tf)
        rows = pl.ds(pl.multiple_of(tile * ts, ts), ts)
        return (
            pltpu.make_async_copy(wup_hbm.at[:, pl.ds(col_a, tf)], wa_buf.at[slot], sem.at[0, slot]),
            pltpu.make_async_copy(wup_hbm.at[:, pl.ds(col_b, tf)], wb_buf.at[slot], sem.at[1, slot]),
            pltpu.make_async_copy(wdn_hbm.at[pl.ds(col_a, tf), :], wd_buf.at[slot], sem.at[2, slot]),
            pltpu.make_async_copy(st_hbm.at[layer, rows, :, pl.ds(col_a, tf)], sa_buf.at[slot], sem.at[3, slot]),
            pltpu.make_async_copy(st_hbm.at[layer, rows, :, pl.ds(col_b, tf)], sb_buf.at[slot], sem.at[4, slot]),
        )

    @pl.when(i == 0)
    def _():
        for cp in chunk_copies(0, 0, 0):
            cp.start()

    h = h_ref[...]
    hs = hs_ref[...]
    hn_ref[0:tm, :] = _rms(h, g_ref[...]).astype(_BF16)
    hn_ref[tm:tm + ts, :] = _rms(hs, g_ref[...]).astype(_BF16)
    o_ref[...] = h
    os_ref[...] = hs

    @pl.when(s == 0)
    def _():
        ca_ref[...] = jnp.zeros(ca_ref.shape, _F32)
        cb_ref[...] = jnp.zeros(cb_ref.shape, _F32)

    starts = list(range(0, tm, sub))
    blocks = [(r0, r0 + sub) for r0 in starts[:-1]] + [(starts[-1], tm + ts)]

    def conv(e_ref, k, r0, r1):
        lo, n = _HALO_SHORT + r0, r1 - r0
        out = k[0:1, :] * e_ref[lo - 2:lo - 2 + n, :]
        out = out + k[1:2, :] * e_ref[lo - 1:lo - 1 + n, :]
        return out + k[2:3, :] * e_ref[lo:lo + n, :]

    def conv_sample(e_ref, k, st):
        new = e_ref[_HALO_SHORT + tm:_HALO_SHORT + tm + ts, :]
        hist = jnp.sum(st * k[0:FFN_K - 1, :][None], axis=1)
        return hist + k[FFN_K - 1:FFN_K, :] * new

    def chunk(c, carry):
        step = i * n_c + c
        slot = step % 2
        for cp in chunk_copies(i, c, slot):
            cp.wait()

        @pl.when(step + 1 < n_tiles * n_c)
        def _():
            wrap = c + 1 == n_c
            for cp in chunk_copies(jnp.where(wrap, i + 1, i), jnp.where(wrap, 0, c + 1), 1 - slot):
                cp.start()

        col_a = pl.multiple_of(c * tf, tf)
        col_b = pl.multiple_of(D_FF + c * tf, tf)
        wa = wa_buf[slot]
        wb = wb_buf[slot]
        wd = wd_buf[slot]
        ka = k_ref[:, pl.ds(col_a, tf)]
        kb = k_ref[:, pl.ds(col_b, tf)]
        ea_ref[0:_HALO_SHORT, :] = ca_ref[c]
        eb_ref[0:_HALO_SHORT, :] = cb_ref[c]

        for r0, r1 in blocks:
            hn = hn_ref[r0:r1, :]
            ea_ref[_HALO_SHORT + r0:_HALO_SHORT + r1, :] = _dot(hn, wa)
            eb_ref[_HALO_SHORT + r0:_HALO_SHORT + r1, :] = _dot(hn, wb)

        for r0, r1 in blocks:
            r1p = min(r1, tm)
            act = _silu(conv(ea_ref, ka, r0, r1p)) * conv(eb_ref, kb, r0, r1p)
            if r1 > tm:
                act_s = (_silu(conv_sample(ea_ref, ka, sa_buf[slot]))
                         * conv_sample(eb_ref, kb, sb_buf[slot]))
                res = _dot(jnp.concatenate([act, act_s], axis=0).astype(_BF16), wd)
                o_ref[r0:tm, :] += res[0:tm - r0, :]
                os_ref[...] += res[tm - r0:tm - r0 + ts, :]
            else:
                o_ref[r0:r1, :] += _dot(act.astype(_BF16), wd)

        ca_ref[c] = ea_ref[tm:tm + _HALO_SHORT, :]
        cb_ref[c] = eb_ref[tm:tm + _HALO_SHORT, :]
        tail = slice(tm + _HALO_SHORT - (FFN_K - 1), tm + _HALO_SHORT)
        nf_ref[:, pl.ds(col_a, tf)] = ea_ref[tail, :]
        nf_ref[:, pl.ds(col_b, tf)] = eb_ref[tail, :]
        new = slice(_HALO_SHORT + tm, _HALO_SHORT + tm + ts)
        us_ref[:, pl.ds(col_a, tf)] = ea_ref[new, :]
        us_ref[:, pl.ds(col_b, tf)] = eb_ref[new, :]
        return carry

    lax.fori_loop(0, n_c, chunk, 0)


def _ffn(h, hs, nb, norm_g, w_up, ffn_conv, w_down, state_ffn, layer, *, tm, tf, sub):
    m = h.shape[0]
    n_tiles = m // tm
    n_s = n_tiles // nb
    n_c = D_FF // tf
    ts = hs.shape[0] // n_tiles
    assert ts * n_tiles == hs.shape[0] and ts % _BF16_ROWS == 0, (hs.shape, n_tiles)
    return pl.pallas_call(
        functools.partial(_ffn_kernel, layer=layer, tm=tm, ts=ts, sub=sub, tf=tf, n_c=n_c,
                          n_s=n_s, n_tiles=n_tiles),
        out_shape=(
            jax.ShapeDtypeStruct((m, D_MODEL), _F32),
            jax.ShapeDtypeStruct(hs.shape, _F32),
            jax.ShapeDtypeStruct((n_tiles, FFN_K - 1, 2 * D_FF), _F32),
            jax.ShapeDtypeStruct((hs.shape[0], 2 * D_FF), _F32),
        ),
        grid=(n_tiles,),
        in_specs=[
            _single((tm, D_MODEL), lambda i: (i, 0)),
            pl.BlockSpec((ts, D_MODEL), lambda i: (i, 0)),
            pl.BlockSpec((None, 1, D_MODEL), lambda i: (layer, 0, 0)),
            pl.BlockSpec((None, FFN_K, 2 * D_FF), lambda i: (layer, 0, 0)),
            pl.BlockSpec(memory_space=pl.ANY),
            pl.BlockSpec(memory_space=pl.ANY),
            pl.BlockSpec(memory_space=pl.ANY),
        ],
        out_specs=(
            pl.BlockSpec((tm, D_MODEL), lambda i: (i, 0)),
            pl.BlockSpec((ts, D_MODEL), lambda i: (i, 0)),
            pl.BlockSpec((None, FFN_K - 1, 2 * D_FF), lambda i: (i, 0, 0)),
            pl.BlockSpec((ts, 2 * D_FF), lambda i: (i, 0)),
        ),
        scratch_shapes=[
            pltpu.VMEM((tm + ts, D_MODEL), _BF16),
            pltpu.VMEM((_HALO_SHORT + tm + ts, tf), _F32),
            pltpu.VMEM((_HALO_SHORT + tm + ts, tf), _F32),
            pltpu.VMEM((n_c, _HALO_SHORT, tf), _F32),
            pltpu.VMEM((n_c, _HALO_SHORT, tf), _F32),
            pltpu.VMEM((2, D_MODEL, tf), _BF16),
            pltpu.VMEM((2, D_MODEL, tf), _BF16),
            pltpu.VMEM((2, tf, D_MODEL), _BF16),
            pltpu.VMEM((2, ts, FFN_K - 1, tf), _F32),
            pltpu.VMEM((2, ts, FFN_K - 1, tf), _F32),
            pltpu.SemaphoreType.DMA((5, 2)),
        ],
        compiler_params=_params("arbitrary"),
        name="ffn",
    )(h, hs, _vec3(norm_g), ffn_conv, state_ffn, w_up, w_down)


def _ple_kernel(h_ref, hs_ref, g_ref, gate_ref, p_ref, ps_ref, proj_ref, gf_ref, o_ref, os_ref,
                gate_bf, proj_bf, *, final, tm, sub, n_i):
    i = pl.program_id(0)

    @pl.when(i == 0)
    def _():
        gate_bf[...] = gate_ref[...].astype(_BF16)
        proj_bf[...] = proj_ref[...].astype(_BF16)

    def rows(h, p):
        hn = _rms(h, g_ref[...]).astype(_BF16)
        gate = jax.nn.sigmoid(_dot(hn, gate_bf[...]))
        out = h + _dot(p.astype(_BF16), proj_bf[...]) * gate
        return _rms(out, gf_ref[...]) if final else out

    @pl.when(i < n_i)
    def _():
        for r0 in range(0, tm, sub):
            o_ref[r0:r0 + sub, :] = rows(h_ref[r0:r0 + sub, :], p_ref[r0:r0 + sub, :])

    @pl.when(i == n_i)
    def _():
        os_ref[...] = rows(hs_ref[...], ps_ref[...])


def _ple(h, hs, norm_g, gate_w, p, ps, proj_w, layer, final_g, *, tm, sub, final):
    m = h.shape[0]
    ms = hs.shape[0]
    n_i = m // tm
    row = lambda i: (jnp.minimum(i, n_i - 1), 0)
    return pl.pallas_call(
        functools.partial(_ple_kernel, final=final, tm=tm, sub=sub, n_i=n_i),
        out_shape=(jax.ShapeDtypeStruct((m, D_MODEL), _F32), jax.ShapeDtypeStruct((ms, D_MODEL), _F32)),
        grid=(n_i + 1,),
        in_specs=[
            pl.BlockSpec((tm, D_MODEL), row),
            pl.BlockSpec((ms, D_MODEL), lambda i: (0, 0)),
            pl.BlockSpec((None, 1, D_MODEL), lambda i: (layer, 0, 0)),
            _single((None, D_MODEL, D_MODEL), lambda i: (layer, 0, 0)),
            pl.BlockSpec((None, tm, PLE_DIM), lambda i: (layer, jnp.minimum(i, n_i - 1), 0)),
            pl.BlockSpec((None, ms, PLE_DIM), lambda i: (layer, 0, 0)),
            _single((None, PLE_DIM, D_MODEL), lambda i: (layer, 0, 0)),
            pl.BlockSpec((1, D_MODEL), lambda i: (0, 0)),
        ],
        out_specs=(
            pl.BlockSpec((tm, D_MODEL), row),
            pl.BlockSpec((ms, D_MODEL), lambda i: (0, 0)),
        ),
        scratch_shapes=[
            pltpu.VMEM((D_MODEL, D_MODEL), _BF16),
            pltpu.VMEM((PLE_DIM, D_MODEL), _BF16),
        ],
        compiler_params=_params("arbitrary"),
        name="ple",
    )(h, hs, _vec3(norm_g), gate_w, p, ps, proj_w, final_g.reshape(1, -1))


def kernel(x_prompt, x_sample, p_prompt, p_sample, state_pool, state_conf, state_sc, state_ffn,
           norm_mix, w_in, w_pool, pool_scale, conf_dw, conf_dw_b, conf_ln_g, conf_ln_b,
           conf_pw, conf_pw_b, sc_conv, w_out, norm_ffn, w_up, ffn_conv, w_down,
           norm_ple, ple_gate, ple_proj, norm_final):
    nb, seq, _ = x_prompt.shape
    ns = x_sample.shape[0]
    depth = w_in.shape[0]
    hp = x_prompt.reshape(nb * seq, D_MODEL)
    hs = x_sample.reshape(ns, D_MODEL)
    pp = p_prompt.reshape(depth, nb * seq, PLE_DIM)
    ps = p_sample.reshape(depth, ns, PLE_DIM)
    mix_w = (w_pool, pool_scale, conf_dw, conf_dw_b, conf_ln_g, conf_ln_b, conf_pw, conf_pw_b, sc_conv)

    prompt_states = [[] for _ in range(4)]
    sample_rows = [[] for _ in range(4)]
    for i in range(depth):
        last = i == depth - 1

        z, zs, w_up_bf, w_down_bf = _norm_matmul(hp, hs, norm_mix, w_in, w_up, w_down, i,
                                                 tm=_TM_IN, tn=_TN_IN, sub=_SUB_IN)
        npool_p = z.reshape(nb, seq, IN_COLS)[:, seq - POOL_BUF:, :POOL_WIDTH]
        h1, nconf_p, nsc_p = _mix_out_prompt(z, hp, nb, i, mix_w, w_out, tm=_TM_MIX)
        h1s, g_s, v_s = _mix_out_sample(zs, hs, state_pool, state_conf, state_sc, i, mix_w, w_out)

        h2, h2s, nf, up_s = _ffn(h1, h1s, nb, norm_ffn, w_up_bf, ffn_conv, w_down_bf,
                                 state_ffn, i, tm=_TM_FFN, tf=_TF_FFN, sub=_SUB_FFN)
        tiles_per_seq = nf.shape[0] // nb
        nffn_p = nf[tiles_per_seq - 1::tiles_per_seq]

        hp, hs = _ple(h2, h2s, norm_ple, ple_gate, pp, ps, ple_proj, i, norm_final,
                      tm=_TM_PLE, sub=_SUB_PLE, final=last)

        for lst, val in zip(prompt_states, (npool_p, nconf_p, nsc_p, nffn_p)):
            lst.append(val)
        for lst, val in zip(sample_rows, (zs[:, :POOL_WIDTH], g_s, v_s, up_s)):
            lst.append(val)

    def shifted(state, rows):
        return jnp.concatenate([state[:, :, 1:], jnp.stack(rows)[:, :, None]], axis=2)

    new_p = [jnp.stack(l) for l in prompt_states]
    new_s = [shifted(st, rows) for st, rows in
             zip((state_pool, state_conf, state_sc, state_ffn), sample_rows)]
    y_prompt = hp.reshape(nb, seq, D_MODEL)
    y_sample = hs.reshape(ns, 1, D_MODEL)
    return (y_prompt, y_sample, new_p[0], new_s[0], new_p[1], new_s[1],
            new_p[2], new_s[2], new_p[3], new_s[3])
```

```python
import functools

import jax
import jax.numpy as jnp
from jax import lax
from jax.experimental import pallas as pl
from jax.experimental.pallas import tpu as pltpu

D_MODEL = 2048
POOL_WIDTH = 512
POOL_WINDOWS = (2, 4, 8, 16)
POOL_GW = 128
POOL_BUF = 15
CONV_WIDTH = 768
CONF_K = 31
SC_WIDTH = 768
SC_K = 3
FFN_K = 3
D_FF = 5632
PLE_DIM = 256
IN_COLS = POOL_WIDTH + 2 * CONV_WIDTH + 3 * SC_WIDTH
EPS = 1e-6
PAST_LEN = 16384

_O_U = 0
_O_GA = _O_U + POOL_WIDTH
_O_GB = _O_GA + CONV_WIDTH
_O_SB = _O_GB + CONV_WIDTH
_O_SC = _O_SB + SC_WIDTH
_O_SH = _O_SC + SC_WIDTH
_Y_A = 0
_Y_B = POOL_WIDTH
_Y_C = POOL_WIDTH + CONV_WIDTH

_VMEM_LIMIT = 56 * 1024 * 1024
_LANE = 128
_SUBLANE = 8
_BF16_ROWS = 16

_HALO_POOL = 16
_HALO_CONF = 32
_HALO_SHORT = 8

_TM_IN, _TN_IN, _SUB_IN = 512, IN_COLS // 2, 128
_STAGE_ROWS = 256
_TM_MIX = 256
_CONV_ROWS = 128
_CONV_BLOCKS_PER_TICK = 3
_OUT_COLS_PER_TICK = 256
_TM_FFN, _TF_FFN, _SUB_FFN = 1024, 512, 256
_TM_PLE, _SUB_PLE = 512, 128

_BF16 = jnp.bfloat16
_F32 = jnp.float32


def _params(*sem):
    return pltpu.CompilerParams(dimension_semantics=sem, vmem_limit_bytes=_VMEM_LIMIT)


def _rms(x, g):
    ms = jnp.mean(x * x, axis=-1, keepdims=True)
    return x * lax.rsqrt(ms + EPS) * g


def _dot(a, b):
    return jnp.dot(a, b, preferred_element_type=_F32)


def _silu(x):
    return x * jax.nn.sigmoid(x)


def _single(block, index_map):
    return pl.BlockSpec(block, index_map, pipeline_mode=pl.Buffered(1))


def _vec3(a):
    return a.reshape(a.shape[0], 1, a.shape[1])


def _norm_matmul_kernel(x_ref, xs_ref, g_ref, w_hbm, wu_ref, wd_ref,
                        o_ref, os_ref, wub_ref, wdb_ref,
                        wb_ref, stage_ref, sem, *, layer, tm, tn, sub, n_i):
    j = pl.program_id(0)
    i = pl.program_id(1)

    @pl.when(i == 0)
    def _():
        rows = stage_ref.shape[1]
        col = pl.multiple_of(j * tn, _LANE)

        def piece(r, slot):
            return pltpu.make_async_copy(w_hbm.at[layer, pl.ds(r * rows, rows), pl.ds(col, tn)],
                                         stage_ref.at[slot], sem.at[slot])

        n = wb_ref.shape[0] // rows
        piece(0, 0).start()
        for r in range(n):
            piece(r, r % 2).wait()
            if r + 1 < n:
                piece(r + 1, (r + 1) % 2).start()
            wb_ref[r * rows:(r + 1) * rows, :] = stage_ref[r % 2].astype(_BF16)

    @pl.when(i < n_i)
    def _():
        for r0 in range(0, tm, sub):
            hn = _rms(x_ref[r0:r0 + sub, :], g_ref[...]).astype(_BF16)
            o_ref[r0:r0 + sub, :] = _dot(hn, wb_ref[...])
        wub_ref[...] = wu_ref[...].astype(_BF16)
        wdb_ref[...] = wd_ref[...].astype(_BF16)

    @pl.when(i == n_i)
    def _():
        os_ref[...] = _dot(_rms(xs_ref[...], g_ref[...]).astype(_BF16), wb_ref[...])


def _norm_matmul(x, xs, g, w, w_up, w_down, layer, *, tm, tn, sub):
    m, k = x.shape
    ms = xs.shape[0]
    n = w.shape[2]
    n_i = m // tm
    n_j = n // tn
    n_cast = n_j * n_i
    ru, rd = w_up.shape[1] // n_cast, w_down.shape[1] // n_cast
    assert ru * n_cast == w_up.shape[1] and rd * n_cast == w_down.shape[1]
    assert ru % _BF16_ROWS == 0 and rd % _BF16_ROWS == 0 and k % _STAGE_ROWS == 0
    row = lambda j, i: (jnp.minimum(i, n_i - 1), 0)
    cast_in = lambda j, i: (layer, j * n_i + jnp.minimum(i, n_i - 1), 0)
    cast_out = lambda j, i: (j * n_i + jnp.minimum(i, n_i - 1), 0)
    return pl.pallas_call(
        functools.partial(_norm_matmul_kernel, layer=layer, tm=tm, tn=tn, sub=sub, n_i=n_i),
        out_shape=(
            jax.ShapeDtypeStruct((m, n), _F32),
            jax.ShapeDtypeStruct((ms, n), _F32),
            jax.ShapeDtypeStruct(w_up.shape[1:], _BF16),
            jax.ShapeDtypeStruct(w_down.shape[1:], _BF16),
        ),
        grid=(n_j, n_i + 1),
        in_specs=[
            pl.BlockSpec((tm, k), row),
            pl.BlockSpec((ms, k), lambda j, i: (0, 0)),
            pl.BlockSpec((None, 1, k), lambda j, i: (layer, 0, 0)),
            pl.BlockSpec(memory_space=pl.ANY),
            pl.BlockSpec((None, ru, w_up.shape[2]), cast_in),
            pl.BlockSpec((None, rd, w_down.shape[2]), cast_in),
        ],
        out_specs=(
            pl.BlockSpec((tm, tn), lambda j, i: (jnp.minimum(i, n_i - 1), j)),
            pl.BlockSpec((ms, tn), lambda j, i: (0, j)),
            pl.BlockSpec((ru, w_up.shape[2]), cast_out),
            pl.BlockSpec((rd, w_down.shape[2]), cast_out),
        ),
        scratch_shapes=[
            pltpu.VMEM((k, tn), _BF16),
            pltpu.VMEM((2, _STAGE_ROWS, tn), _F32),
            pltpu.SemaphoreType.DMA((2,)),
        ],
        compiler_params=_params("arbitrary", "arbitrary"),
        name="in_proj",
    )(x, xs, _vec3(g), w, w_up, w_down)


def _mixer_weight_specs(layer):
    l3 = lambda *_: (layer, 0, 0)
    l4 = lambda *_: (layer, 0, 0, 0)
    return [
        _single((None, len(POOL_WINDOWS), POOL_GW, POOL_GW), l4),
        pl.BlockSpec((None, 1, POOL_WIDTH), l3),
        pl.BlockSpec((None, CONF_K, CONV_WIDTH), l3),
        pl.BlockSpec((None, 1, CONV_WIDTH), l3),
        pl.BlockSpec((None, 1, CONV_WIDTH), l3),
        pl.BlockSpec((None, 1, CONV_WIDTH), l3),
        _single((None, CONV_WIDTH, CONV_WIDTH), l3),
        pl.BlockSpec((None, 1, CONV_WIDTH), l3),
        pl.BlockSpec((None, SC_K, SC_WIDTH), l3),
    ]


def _mixer_weight_args(w_pool, pool_scale, conf_dw, conf_dw_b, conf_ln_g, conf_ln_b,
                       conf_pw, conf_pw_b, sc_conv):
    return (w_pool, _vec3(pool_scale), conf_dw, _vec3(conf_dw_b), _vec3(conf_ln_g),
            _vec3(conf_ln_b), conf_pw, _vec3(conf_pw_b), sc_conv)


def _layernorm_silu(cb, lng_ref, lnb_ref):
    mu = jnp.mean(cb, axis=-1, keepdims=True)
    xc = cb - mu
    var = jnp.mean(xc * xc, axis=-1, keepdims=True)
    return _silu(xc * lax.rsqrt(var + EPS) * lng_ref[...] + lnb_ref[...])


def _conv31_block(eg_ref, dw_ref, c0, t0, rows):
    base = _HALO_CONF - (CONF_K - 1)
    acc = None
    for r in range(_SUBLANE):
        p = None
        for q in range((CONF_K + base) // _SUBLANE + 1):
            k = _SUBLANE * q + r - base
            if 0 <= k < CONF_K:
                lo = t0 + _SUBLANE * q
                term = dw_ref[k:k + 1, c0:c0 + _LANE] * eg_ref[lo:lo + rows + _SUBLANE, c0:c0 + _LANE]
                p = term if p is None else p + term
        shifted = p[r:r + rows, :]
        acc = shifted if acc is None else acc + shifted
    return acc


def _mixers_prompt(z_ref, wpool_ref, pscale_ref, dw_ref, dwb_ref, lng_ref, lnb_ref,
                   pwbf_ref, pwb_ref, scw_ref, y_ref, nconf_ref, nsc_ref,
                   eu_ref, eg_ref, ev_ref, cb_ref, *, tm, s, tick):
    tick()
    ev_ref[_HALO_SHORT:_HALO_SHORT + tm, :] = (
        z_ref[:, _O_SC:_O_SC + SC_WIDTH] * z_ref[:, _O_SH:_O_SH + SC_WIDTH])
    cv = scw_ref[0:1, :] * ev_ref[_HALO_SHORT - 2:_HALO_SHORT - 2 + tm, :]
    cv = cv + scw_ref[1:2, :] * ev_ref[_HALO_SHORT - 1:_HALO_SHORT - 1 + tm, :]
    cv = cv + scw_ref[2:3, :] * ev_ref[_HALO_SHORT:_HALO_SHORT + tm, :]
    y_ref[:, _Y_C:_Y_C + SC_WIDTH] = (z_ref[:, _O_SB:_O_SB + SC_WIDTH] * cv).astype(_BF16)
    nsc_ref[...] = ev_ref[tm + _HALO_SHORT - (SC_K - 1):tm + _HALO_SHORT, :]
    ev_ref[0:_HALO_SHORT, :] = ev_ref[tm:tm + _HALO_SHORT, :]

    eu_ref[_HALO_POOL:_HALO_POOL + tm, :] = z_ref[:, _O_U:_O_U + POOL_WIDTH]
    pos = s * tm + lax.broadcasted_iota(jnp.int32, (tm, 1), 0)
    for g, w in enumerate(POOL_WINDOWS):
        if g % 2 == 0:
            tick()
        c0 = g * POOL_GW
        u = eu_ref[_HALO_POOL:_HALO_POOL + tm, c0:c0 + POOL_GW]
        wsum = u
        for k in range(1, w):
            wsum = wsum + eu_ref[_HALO_POOL - k:_HALO_POOL - k + tm, c0:c0 + POOL_GW]
        cnt = jnp.minimum(w, pos + 1).astype(_F32)
        d = wsum / cnt - u
        y_ref[:, _Y_A + c0:_Y_A + c0 + POOL_GW] = (
            _dot(d, wpool_ref[g]) * pscale_ref[:, c0:c0 + POOL_GW]).astype(_BF16)
    eu_ref[0:_HALO_POOL, :] = eu_ref[tm:tm + _HALO_POOL, :]

    eg_ref[_HALO_CONF:_HALO_CONF + tm, :] = (
        z_ref[:, _O_GA:_O_GA + CONV_WIDTH] * jax.nn.sigmoid(z_ref[:, _O_GB:_O_GB + CONV_WIDTH]))
    blocks = [(t0, c0) for t0 in range(0, tm, _CONV_ROWS) for c0 in range(0, CONV_WIDTH, _LANE)]
    for n, (t0, c0) in enumerate(blocks):
        if n % _CONV_BLOCKS_PER_TICK == 0:
            tick()
        cb_ref[t0:t0 + _CONV_ROWS, c0:c0 + _LANE] = (
            _conv31_block(eg_ref, dw_ref, c0, t0, _CONV_ROWS) + dwb_ref[:, c0:c0 + _LANE])
    tick()
    act = _layernorm_silu(cb_ref[...], lng_ref, lnb_ref).astype(_BF16)
    y_ref[:, _Y_B:_Y_B + CONV_WIDTH] = (_dot(act, pwbf_ref[...]) + pwb_ref[...]).astype(_BF16)
    nconf_ref[...] = eg_ref[tm + _HALO_CONF - (CONF_K - 1):tm + _HALO_CONF, :]
    eg_ref[0:_HALO_CONF, :] = eg_ref[tm:tm + _HALO_CONF, :]


def _mix_out_prompt_kernel(z_ref, h_ref, wpool_ref, pscale_ref, dw_ref, dwb_ref, lng_ref, lnb_ref,
                           pw_ref, pwb_ref, scw_ref, wout_ref,
                           o_ref, nconf_ref, nsc_ref,
                           eu_ref, eg_ref, ev_ref, cb_ref, y0_ref, y1_ref, wo_ref, pwbf_ref,
                           *, tm, n_s, n_tiles):
    i = pl.program_id(0)
    s = jnp.minimum(i, n_tiles - 1) % n_s

    @pl.when(i == 0)
    def _():
        wo_ref[...] = wout_ref[...].astype(_BF16)
        pwbf_ref[...] = pw_ref[...].astype(_BF16)
        eg_ref[_HALO_CONF + tm:_HALO_CONF + tm + _SUBLANE, :] = jnp.zeros((_SUBLANE, CONV_WIDTH), _F32)
        y1_ref[...] = jnp.zeros(y1_ref.shape, _BF16)

    @pl.when(s == 0)
    def _():
        eu_ref[0:_HALO_POOL, :] = jnp.zeros((_HALO_POOL, POOL_WIDTH), _F32)
        eg_ref[0:_HALO_CONF, :] = jnp.zeros((_HALO_CONF, CONV_WIDTH), _F32)
        ev_ref[0:_HALO_SHORT, :] = jnp.zeros((_HALO_SHORT, SC_WIDTH), _F32)

    def step(y_prev_ref, y_cur_ref):
        pending = list(range(0, D_MODEL, _OUT_COLS_PER_TICK))

        def tick():
            if pending:
                c0 = pending.pop(0)
                cols = slice(c0, c0 + _OUT_COLS_PER_TICK)
                o_ref[:, cols] = h_ref[:, cols] + _dot(y_prev_ref[...], wo_ref[:, cols])

        _mixers_prompt(z_ref, wpool_ref, pscale_ref, dw_ref, dwb_ref, lng_ref, lnb_ref,
                       pwbf_ref, pwb_ref, scw_ref, y_cur_ref, nconf_ref, nsc_ref,
                       eu_ref, eg_ref, ev_ref, cb_ref, tm=tm, s=s, tick=tick)
        while pending:
            tick()

    @pl.when(i % 2 == 0)
    def _():
        step(y1_ref, y0_ref)

    @pl.when(i % 2 == 1)
    def _():
        step(y0_ref, y1_ref)


def _mix_out_prompt(z, h, nb, layer, mix_w, w_out, *, tm):
    m = z.shape[0]
    n_tiles = m // tm
    n_s = n_tiles // nb
    cur = lambda i: (jnp.minimum(i, n_tiles - 1), 0)
    prev = lambda i: (jnp.maximum(i - 1, 0), 0)
    seq = lambda i: (jnp.minimum(i, n_tiles - 1) // n_s, 0, 0)
    return pl.pallas_call(
        functools.partial(_mix_out_prompt_kernel, tm=tm, n_s=n_s, n_tiles=n_tiles),
        out_shape=(
            jax.ShapeDtypeStruct((m, D_MODEL), _F32),
            jax.ShapeDtypeStruct((nb, CONF_K - 1, CONV_WIDTH), _F32),
            jax.ShapeDtypeStruct((nb, SC_K - 1, SC_WIDTH), _F32),
        ),
        grid=(n_tiles + 1,),
        in_specs=([pl.BlockSpec((tm, IN_COLS), cur), pl.BlockSpec((tm, D_MODEL), prev)]
                  + _mixer_weight_specs(layer)
                  + [_single((None, D_MODEL, D_MODEL), lambda i: (layer, 0, 0))]),
        out_specs=(
            pl.BlockSpec((tm, D_MODEL), prev),
            pl.BlockSpec((None, CONF_K - 1, CONV_WIDTH), seq),
            pl.BlockSpec((None, SC_K - 1, SC_WIDTH), seq),
        ),
        scratch_shapes=[
            pltpu.VMEM((_HALO_POOL + tm, POOL_WIDTH), _F32),
            pltpu.VMEM((_HALO_CONF + tm + _SUBLANE, CONV_WIDTH), _F32),
            pltpu.VMEM((_HALO_SHORT + tm, SC_WIDTH), _F32),
            pltpu.VMEM((tm, CONV_WIDTH), _F32),
            pltpu.VMEM((tm, D_MODEL), _BF16),
            pltpu.VMEM((tm, D_MODEL), _BF16),
            pltpu.VMEM((D_MODEL, D_MODEL), _BF16),
            pltpu.VMEM((CONV_WIDTH, CONV_WIDTH), _BF16),
        ],
        compiler_params=_params("arbitrary"),
        name="mix_out_prompt",
    )(z, h, *_mixer_weight_args(*mix_w), w_out)


def _mix_out_sample_kernel(z_ref, h_ref, spool_ref, sconf_ref, ssc_ref,
                           wpool_ref, pscale_ref, dw_ref, dwb_ref, lng_ref, lnb_ref,
                           pw_ref, pwb_ref, scw_ref, wout_ref,
                           o_ref, g_ref, v_ref):
    out = h_ref[...]
    for g, w in enumerate(POOL_WINDOWS):
        c0 = g * POOL_GW
        u = z_ref[:, _O_U + c0:_O_U + c0 + POOL_GW]
        wsum = u + jnp.sum(spool_ref[:, POOL_BUF - (w - 1):POOL_BUF, c0:c0 + POOL_GW], axis=1)
        d = wsum / jnp.float32(min(w, PAST_LEN + 1)) - u
        y_a = _dot(d, wpool_ref[g]) * pscale_ref[:, c0:c0 + POOL_GW]
        out = out + _dot(y_a, wout_ref[_Y_A + c0:_Y_A + c0 + POOL_GW, :])

    gl = z_ref[:, _O_GA:_O_GA + CONV_WIDTH] * jax.nn.sigmoid(z_ref[:, _O_GB:_O_GB + CONV_WIDTH])
    g_ref[...] = gl
    acc = dw_ref[CONF_K - 1:CONF_K, :] * gl
    acc = acc + jnp.sum(sconf_ref[...] * dw_ref[0:CONF_K - 1, :][None], axis=1)
    y_b = _dot(_layernorm_silu(acc + dwb_ref[...], lng_ref, lnb_ref), pw_ref[...]) + pwb_ref[...]
    out = out + _dot(y_b, wout_ref[_Y_B:_Y_B + CONV_WIDTH, :])

    v = z_ref[:, _O_SC:_O_SC + SC_WIDTH] * z_ref[:, _O_SH:_O_SH + SC_WIDTH]
    v_ref[...] = v
    cv = jnp.sum(ssc_ref[...] * scw_ref[0:SC_K - 1, :][None], axis=1) + scw_ref[SC_K - 1:SC_K, :] * v
    y_c = z_ref[:, _O_SB:_O_SB + SC_WIDTH] * cv
    o_ref[...] = out + _dot(y_c, wout_ref[_Y_C:_Y_C + SC_WIDTH, :])


def _mix_out_sample(z, h, state_pool, state_conf, state_sc, layer, mix_w, w_out):
    m = z.shape[0]
    full = lambda a: pl.BlockSpec(a.shape, lambda i: (0,) * a.ndim)
    state = lambda a: _single((None,) + a.shape[1:], lambda i: (layer, 0, 0, 0))
    return pl.pallas_call(
        _mix_out_sample_kernel,
        out_shape=(
            jax.ShapeDtypeStruct((m, D_MODEL), _F32),
            jax.ShapeDtypeStruct((m, CONV_WIDTH), _F32),
            jax.ShapeDtypeStruct((m, SC_WIDTH), _F32),
        ),
        grid=(1,),
        in_specs=([full(z), full(h), state(state_pool), state(state_conf), state(state_sc)]
                  + _mixer_weight_specs(layer)
                  + [_single((None, D_MODEL, D_MODEL), lambda i: (layer, 0, 0))]),
        out_specs=(
            pl.BlockSpec((m, D_MODEL), lambda i: (0, 0)),
            pl.BlockSpec((m, CONV_WIDTH), lambda i: (0, 0)),
            pl.BlockSpec((m, SC_WIDTH), lambda i: (0, 0)),
        ),
        compiler_params=_params("arbitrary"),
        name="mix_out_sample",
    )(z, h, state_pool, state_conf, state_sc, *_mixer_weight_args(*mix_w), w_out)


def _ffn_kernel(h_ref, hs_ref, g_ref, k_ref, st_hbm, wup_hbm, wdn_hbm,
                o_ref, os_ref, nf_ref, us_ref,
                hn_ref, ea_ref, eb_ref, ca_ref, cb_ref, wa_buf, wb_buf, wd_buf, sa_buf, sb_buf, sem,
                *, layer, tm, ts, sub, tf, n_c, n_s, n_tiles):
    i = pl.program_id(0)
    s = i % n_s

    def chunk_copies(tile, chunk, slot):
        col_a = pl.multiple_of(chunk * tf, tf)
        col_b = pl.multiple_of(D_FF + chunk * tf, tf)
        rows = pl.ds(pl.multiple_of(tile * ts, ts), ts)
        return (
            pltpu.make_async_copy(wup_hbm.at[:, pl.ds(col_a, tf)], wa_buf.at[slot], sem.at[0, slot]),
            pltpu.make_async_copy(wup_hbm.at[:, pl.ds(col_b, tf)], wb_buf.at[slot], sem.at[1, slot]),
            pltpu.make_async_copy(wdn_hbm.at[pl.ds(col_a, tf), :], wd_buf.at[slot], sem.at[2, slot]),
            pltpu.make_async_copy(st_hbm.at[layer, rows, :, pl.ds(col_a, tf)], sa_buf.at[slot], sem.at[3, slot]),
            pltpu.make_async_copy(st_hbm.at[layer, rows, :, pl.ds(col_b, tf)], sb_buf.at[slot], sem.at[4, slot]),
        )

    @pl.when(i == 0)
    def _():
        for cp in chunk_copies(0, 0, 0):
            cp.start()

    h = h_ref[...]
    hs = hs_ref[...]
    hn_ref[0:tm, :] = _rms(h, g_ref[...]).astype(_BF16)
    hn_ref[tm:tm + ts, :] = _rms(hs, g_ref[...]).astype(_BF16)
    o_ref[...] = h
    os_ref[...] = hs

    @pl.when(s == 0)
    def _():
        ca_ref[...] = jnp.zeros(ca_ref.shape, _F32)
        cb_ref[...] = jnp.zeros(cb_ref.shape, _F32)

    starts = list(range(0, tm, sub))
    blocks = [(r0, r0 + sub) for r0 in starts[:-1]] + [(starts[-1], tm + ts)]

    def conv(e_ref, k, r0, r1):
        lo, n = _HALO_SHORT + r0, r1 - r0
        out = k[0:1, :] * e_ref[lo - 2:lo - 2 + n, :]
        out = out + k[1:2, :] * e_ref[lo - 1:lo - 1 + n, :]
        return out + k[2:3, :] * e_ref[lo:lo + n, :]

    def conv_sample(e_ref, k, st):
        new = e_ref[_HALO_SHORT + tm:_HALO_SHORT + tm + ts, :]
        hist = jnp.sum(st * k[0:FFN_K - 1, :][None], axis=1)
        return hist + k[FFN_K - 1:FFN_K, :] * new

    def chunk(c, carry):
        step = i * n_c + c
        slot = step % 2
        for cp in chunk_copies(i, c, slot):
            cp.wait()

        @pl.when(step + 1 < n_tiles * n_c)
        def _():
            wrap = c + 1 == n_c
            for cp in chunk_copies(jnp.where(wrap, i + 1, i), jnp.where(wrap, 0, c + 1), 1 - slot):
                cp.start()

        col_a = pl.multiple_of(c * tf, tf)
        col_b = pl.multiple_of(D_FF + c * tf, tf)
        wa = wa_buf[slot]
        wb = wb_buf[slot]
        wd = wd_buf[slot]
        ka = k_ref[:, pl.ds(col_a, tf)]
        kb = k_ref[:, pl.ds(col_b, tf)]
        ea_ref[0:_HALO_SHORT, :] = ca_ref[c]
        eb_ref[0:_HALO_SHORT, :] = cb_ref[c]

        for r0, r1 in blocks:
            hn = hn_ref[r0:r1, :]
            ea_ref[_HALO_SHORT + r0:_HALO_SHORT + r1, :] = _dot(hn, wa)
            eb_ref[_HALO_SHORT + r0:_HALO_SHORT + r1, :] = _dot(hn, wb)

        for r0, r1 in blocks:
            r1p = min(r1, tm)
            act = _silu(conv(ea_ref, ka, r0, r1p)) * conv(eb_ref, kb, r0, r1p)
            if r1 > tm:
                act_s = (_silu(conv_sample(ea_ref, ka, sa_buf[slot]))
                         * conv_sample(eb_ref, kb, sb_buf[slot]))
                res = _dot(jnp.concatenate([act, act_s], axis=0).astype(_BF16), wd)
                o_ref[r0:tm, :] += res[0:tm - r0, :]
                os_ref[...] += res[tm - r0:tm - r0 + ts, :]
            else:
                o_ref[r0:r1, :] += _dot(act.astype(_BF16), wd)

        ca_ref[c] = ea_ref[tm:tm + _HALO_SHORT, :]
        cb_ref[c] = eb_ref[tm:tm + _HALO_SHORT, :]
        tail = slice(tm + _HALO_SHORT - (FFN_K - 1), tm + _HALO_SHORT)
        nf_ref[:, pl.ds(col_a, tf)] = ea_ref[tail, :]
        nf_ref[:, pl.ds(col_b, tf)] = eb_ref[tail, :]
        new = slice(_HALO_SHORT + tm, _HALO_SHORT + tm + ts)
        us_ref[:, pl.ds(col_a, tf)] = ea_ref[new, :]
        us_ref[:, pl.ds(col_b, tf)] = eb_ref[new, :]
        return carry

    lax.fori_loop(0, n_c, chunk, 0)


def _ffn(h, hs, nb, norm_g, w_up, ffn_conv, w_down, state_ffn, layer, *, tm, tf, sub):
    m = h.shape[0]
    n_tiles = m // tm
    n_s = n_tiles // nb
    n_c = D_FF // tf
    ts = hs.shape[0] // n_tiles
    assert ts * n_tiles == hs.shape[0] and ts % _BF16_ROWS == 0, (hs.shape, n_tiles)
    return pl.pallas_call(
        functools.partial(_ffn_kernel, layer=layer, tm=tm, ts=ts, sub=sub, tf=tf, n_c=n_c,
                          n_s=n_s, n_tiles=n_tiles),
        out_shape=(
            jax.ShapeDtypeStruct((m, D_MODEL), _F32),
            jax.ShapeDtypeStruct(hs.shape, _F32),
            jax.ShapeDtypeStruct((n_tiles, FFN_K - 1, 2 * D_FF), _F32),
            jax.ShapeDtypeStruct((hs.shape[0], 2 * D_FF), _F32),
        ),
        grid=(n_tiles,),
        in_specs=[
            _single((tm, D_MODEL), lambda i: (i, 0)),
            pl.BlockSpec((ts, D_MODEL), lambda i: (i, 0)),
            pl.BlockSpec((None, 1, D_MODEL), lambda i: (layer, 0, 0)),
            pl.BlockSpec((None, FFN_K, 2 * D_FF), lambda i: (layer, 0, 0)),
            pl.BlockSpec(memory_space=pl.ANY),
            pl.BlockSpec(memory_space=pl.ANY),
            pl.BlockSpec(memory_space=pl.ANY),
        ],
        out_specs=(
            pl.BlockSpec((tm, D_MODEL), lambda i: (i, 0)),
            pl.BlockSpec((ts, D_MODEL), lambda i: (i, 0)),
            pl.BlockSpec((None, FFN_K - 1, 2 * D_FF), lambda i: (i, 0, 0)),
            pl.BlockSpec((ts, 2 * D_FF), lambda i: (i, 0)),
        ),
        scratch_shapes=[
            pltpu.VMEM((tm + ts, D_MODEL), _BF16),
            pltpu.VMEM((_HALO_SHORT + tm + ts, tf), _F32),
            pltpu.VMEM((_HALO_SHORT + tm + ts, tf), _F32),
            pltpu.VMEM((n_c, _HALO_SHORT, tf), _F32),
            pltpu.VMEM((n_c, _HALO_SHORT, tf), _F32),
            pltpu.VMEM((2, D_MODEL, tf), _BF16),
            pltpu.VMEM((2, D_MODEL, tf), _BF16),
            pltpu.VMEM((2, tf, D_MODEL), _BF16),
            pltpu.VMEM((2, ts, FFN_K - 1, tf), _F32),
            pltpu.VMEM((2, ts, FFN_K - 1, tf), _F32),
            pltpu.SemaphoreType.DMA((5, 2)),
        ],
        compiler_params=_params("arbitrary"),
        name="ffn",
    )(h, hs, _vec3(norm_g), ffn_conv, state_ffn, w_up, w_down)


def _ple_kernel(h_ref, hs_ref, g_ref, gate_ref, p_ref, ps_ref, proj_ref, gf_ref, o_ref, os_ref,
                gate_bf, proj_bf, *, final, tm, sub, n_i):
    i = pl.program_id(0)

    @pl.when(i == 0)
    def _():
        gate_bf[...] = gate_ref[...].astype(_BF16)
        proj_bf[...] = proj_ref[...].astype(_BF16)

    def rows(h, p):
        hn = _rms(h, g_ref[...]).astype(_BF16)
        gate = jax.nn.sigmoid(_dot(hn, gate_bf[...]))
        out = h + _dot(p.astype(_BF16), proj_bf[...]) * gate
        return _rms(out, gf_ref[...]) if final else out

    @pl.when(i < n_i)
    def _():
        for r0 in range(0, tm, sub):
            o_ref[r0:r0 + sub, :] = rows(h_ref[r0:r0 + sub, :], p_ref[r0:r0 + sub, :])

    @pl.when(i == n_i)
    def _():
        os_ref[...] = rows(hs_ref[...], ps_ref[...])


def _ple(h, hs, norm_g, gate_w, p, ps, proj_w, layer, final_g, *, tm, sub, final):
    m = h.shape[0]
    ms = hs.shape[0]
    n_i = m // tm
    row = lambda i: (jnp.minimum(i, n_i - 1), 0)
    return pl.pallas_call(
        functools.partial(_ple_kernel, final=final, tm=tm, sub=sub, n_i=n_i),
        out_shape=(jax.ShapeDtypeStruct((m, D_MODEL), _F32), jax.ShapeDtypeStruct((ms, D_MODEL), _F32)),
        grid=(n_i + 1,),
        in_specs=[
            pl.BlockSpec((tm, D_MODEL), row),
            pl.BlockSpec((ms, D_MODEL), lambda i: (0, 0)),
            pl.BlockSpec((None, 1, D_MODEL), lambda i: (layer, 0, 0)),
            _single((None, D_MODEL, D_MODEL), lambda i: (layer, 0, 0)),
            pl.BlockSpec((None, tm, PLE_DIM), lambda i: (layer, jnp.minimum(i, n_i - 1), 0)),
            pl.BlockSpec((None, ms, PLE_DIM), lambda i: (layer, 0, 0)),
            _single((None, PLE_DIM, D_MODEL), lambda i: (layer, 0, 0)),
            pl.BlockSpec((1, D_MODEL), lambda i: (0, 0)),
        ],
        out_specs=(
            pl.BlockSpec((tm, D_MODEL), row),
            pl.BlockSpec((ms, D_MODEL), lambda i: (0, 0)),
        ),
        scratch_shapes=[
            pltpu.VMEM((D_MODEL, D_MODEL), _BF16),
            pltpu.VMEM((PLE_DIM, D_MODEL), _BF16),
        ],
        compiler_params=_params("arbitrary"),
        name="ple",
    )(h, hs, _vec3(norm_g), gate_w, p, ps, proj_w, final_g.reshape(1, -1))


def kernel(x_prompt, x_sample, p_prompt, p_sample, state_pool, state_conf, state_sc, state_ffn,
           norm_mix, w_in, w_pool, pool_scale, conf_dw, conf_dw_b, conf_ln_g, conf_ln_b,
           conf_pw, conf_pw_b, sc_conv, w_out, norm_ffn, w_up, ffn_conv, w_down,
           norm_ple, ple_gate, ple_proj, norm_final):
    nb, seq, _ = x_prompt.shape
    ns = x_sample.shape[0]
    depth = w_in.shape[0]
    hp = x_prompt.reshape(nb * seq, D_MODEL)
    hs = x_sample.reshape(ns, D_MODEL)
    pp = p_prompt.reshape(depth, nb * seq, PLE_DIM)
    ps = p_sample.reshape(depth, ns, PLE_DIM)
    mix_w = (w_pool, pool_scale, conf_dw, conf_dw_b, conf_ln_g, conf_ln_b, conf_pw, conf_pw_b, sc_conv)

    prompt_states = [[] for _ in range(4)]
    sample_rows = [[] for _ in range(4)]
    for i in range(depth):
        last = i == depth - 1

        z, zs, w_up_bf, w_down_bf = _norm_matmul(hp, hs, norm_mix, w_in, w_up, w_down, i,
                                                 tm=_TM_IN, tn=_TN_IN, sub=_SUB_IN)
        npool_p = z.reshape(nb, seq, IN_COLS)[:, seq - POOL_BUF:, :POOL_WIDTH]
        h1, nconf_p, nsc_p = _mix_out_prompt(z, hp, nb, i, mix_w, w_out, tm=_TM_MIX)
        h1s, g_s, v_s = _mix_out_sample(zs, hs, state_pool, state_conf, state_sc, i, mix_w, w_out)

        h2, h2s, nf, up_s = _ffn(h1, h1s, nb, norm_ffn, w_up_bf, ffn_conv, w_down_bf,
                                 state_ffn, i, tm=_TM_FFN, tf=_TF_FFN, sub=_SUB_FFN)
        tiles_per_seq = nf.shape[0] // nb
        nffn_p = nf[tiles_per_seq - 1::tiles_per_seq]

        hp, hs = _ple(h2, h2s, norm_ple, ple_gate, pp, ps, ple_proj, i, norm_final,
                      tm=_TM_PLE, sub=_SUB_PLE, final=last)

        for lst, val in zip(prompt_states, (npool_p, nconf_p, nsc_p, nffn_p)):
            lst.append(val)
        for lst, val in zip(sample_rows, (zs[:, :POOL_WIDTH], g_s, v_s, up_s)):
            lst.append(val)

    def shifted(state, rows):
        return jnp.concatenate([state[:, :, 1:], jnp.stack(rows)[:, :, None]], axis=2)

    new_p = [jnp.stack(l) for l in prompt_states]
    new_s = [shifted(st, rows) for st, rows in
             zip((state_pool, state_conf, state_sc, state_ffn), sample_rows)]
    y_prompt = hp.reshape(nb, seq, D_MODEL)
    y_sample = hs.reshape(ns, 1, D_MODEL)
    return (y_prompt, y_sample, new_p[0], new_s[0], new_p[1], new_s[1],
            new_p[2], new_s[2], new_p[3], new_s[3])
```

```python
import functools

import jax
import jax.numpy as jnp
from jax import lax
from jax.experimental import pallas as pl
from jax.experimental.pallas import tpu as pltpu

D_MODEL = 2048
POOL_WIDTH = 512
POOL_WINDOWS = (2, 4, 8, 16)
POOL_GW = 128
POOL_BUF = 15
CONV_WIDTH = 768
CONF_K = 31
SC_WIDTH = 768
SC_K = 3
FFN_K = 3
D_FF = 5632
PLE_DIM = 256
IN_COLS = POOL_WIDTH + 2 * CONV_WIDTH + 3 * SC_WIDTH
EPS = 1e-6
PAST_LEN = 16384

_O_U = 0
_O_GA = _O_U + POOL_WIDTH
_O_GB = _O_GA + CONV_WIDTH
_O_SB = _O_GB + CONV_WIDTH
_O_SC = _O_SB + SC_WIDTH
_O_SH = _O_SC + SC_WIDTH
_Y_A = 0
_Y_B = POOL_WIDTH
_Y_C = POOL_WIDTH + CONV_WIDTH

_VMEM_LIMIT = 56 * 1024 * 1024
_LANE = 128
_SUBLANE = 8
_BF16_ROWS = 16

_HALO_POOL = 16
_HALO_CONF = 32
_HALO_SHORT = 8

_TM_IN, _TN_IN, _SUB_IN = 512, IN_COLS // 2, 128
_STAGE_ROWS = 256
_TM_MIX = 256
_CONV_ROWS = 128
_CONV_BLOCKS_PER_TICK = 3
_OUT_COLS_PER_TICK = 256
_TM_FFN, _TF_FFN, _SUB_FFN = 1024, 512, 256
_TM_PLE, _SUB_PLE = 512, 256

_BF16 = jnp.bfloat16
_F32 = jnp.float32


def _params(*sem):
    return pltpu.CompilerParams(dimension_semantics=sem, vmem_limit_bytes=_VMEM_LIMIT)


def _rms(x, g):
    ms = jnp.mean(x * x, axis=-1, keepdims=True)
    return x * lax.rsqrt(ms + EPS) * g


def _dot(a, b):
    return jnp.dot(a, b, preferred_element_type=_F32)


def _silu(x):
    return x * jax.nn.sigmoid(x)


def _single(block, index_map):
    return pl.BlockSpec(block, index_map, pipeline_mode=pl.Buffered(1))


def _vec3(a):
    return a.reshape(a.shape[0], 1, a.shape[1])


def _norm_matmul_kernel(x_ref, xs_ref, g_ref, w_hbm, wu_ref, wd_ref,
                        o_ref, os_ref, wub_ref, wdb_ref,
                        wb_ref, stage_ref, sem, *, layer, tm, tn, sub, n_i):
    j = pl.program_id(0)
    i = pl.program_id(1)

    @pl.when(i == 0)
    def _():
        rows = stage_ref.shape[1]
        col = pl.multiple_of(j * tn, _LANE)

        def piece(r, slot):
            return pltpu.make_async_copy(w_hbm.at[layer, pl.ds(r * rows, rows), pl.ds(col, tn)],
                                         stage_ref.at[slot], sem.at[slot])

        n = wb_ref.shape[0] // rows
        piece(0, 0).start()
        for r in range(n):
            piece(r, r % 2).wait()
            if r + 1 < n:
                piece(r + 1, (r + 1) % 2).start()
            wb_ref[r * rows:(r + 1) * rows, :] = stage_ref[r % 2].astype(_BF16)

    @pl.when(i < n_i)
    def _():
        for r0 in range(0, tm, sub):
            hn = _rms(x_ref[r0:r0 + sub, :], g_ref[...]).astype(_BF16)
            o_ref[r0:r0 + sub, :] = _dot(hn, wb_ref[...])
        wub_ref[...] = wu_ref[...].astype(_BF16)
        wdb_ref[...] = wd_ref[...].astype(_BF16)

    @pl.when(i == n_i)
    def _():
        os_ref[...] = _dot(_rms(xs_ref[...], g_ref[...]).astype(_BF16), wb_ref[...])


def _norm_matmul(x, xs, g, w, w_up, w_down, layer, *, tm, tn, sub):
    m, k = x.shape
    ms = xs.shape[0]
    n = w.shape[2]
    n_i = m // tm
    n_j = n // tn
    n_cast = n_j * n_i
    ru, rd = w_up.shape[1] // n_cast, w_down.shape[1] // n_cast
    assert ru * n_cast == w_up.shape[1] and rd * n_cast == w_down.shape[1]
    assert ru % _BF16_ROWS == 0 and rd % _BF16_ROWS == 0 and k % _STAGE_ROWS == 0
    row = lambda j, i: (jnp.minimum(i, n_i - 1), 0)
    cast_in = lambda j, i: (layer, j * n_i + jnp.minimum(i, n_i - 1), 0)
    cast_out = lambda j, i: (j * n_i + jnp.minimum(i, n_i - 1), 0)
    return pl.pallas_call(
        functools.partial(_norm_matmul_kernel, layer=layer, tm=tm, tn=tn, sub=sub, n_i=n_i),
        out_shape=(
            jax.ShapeDtypeStruct((m, n), _F32),
            jax.ShapeDtypeStruct((ms, n), _F32),
            jax.ShapeDtypeStruct(w_up.shape[1:], _BF16),
            jax.ShapeDtypeStruct(w_down.shape[1:], _BF16),
        ),
        grid=(n_j, n_i + 1),
        in_specs=[
            pl.BlockSpec((tm, k), row),
            pl.BlockSpec((ms, k), lambda j, i: (0, 0)),
            pl.BlockSpec((None, 1, k), lambda j, i: (layer, 0, 0)),
            pl.BlockSpec(memory_space=pl.ANY),
            pl.BlockSpec((None, ru, w_up.shape[2]), cast_in),
            pl.BlockSpec((None, rd, w_down.shape[2]), cast_in),
        ],
        out_specs=(
            pl.BlockSpec((tm, tn), lambda j, i: (jnp.minimum(i, n_i - 1), j)),
            pl.BlockSpec((ms, tn), lambda j, i: (0, j)),
            pl.BlockSpec((ru, w_up.shape[2]), cast_out),
            pl.BlockSpec((rd, w_down.shape[2]), cast_out),
        ),
        scratch_shapes=[
            pltpu.VMEM((k, tn), _BF16),
            pltpu.VMEM((2, _STAGE_ROWS, tn), _F32),
            pltpu.SemaphoreType.DMA((2,)),
        ],
        compiler_params=_params("arbitrary", "arbitrary"),
        name="in_proj",
    )(x, xs, _vec3(g), w, w_up, w_down)


def _mixer_weight_specs(layer):
    l3 = lambda *_: (layer, 0, 0)
    l4 = lambda *_: (layer, 0, 0, 0)
    return [
        _single((None, len(POOL_WINDOWS), POOL_GW, POOL_GW), l4),
        pl.BlockSpec((None, 1, POOL_WIDTH), l3),
        pl.BlockSpec((None, CONF_K, CONV_WIDTH), l3),
        pl.BlockSpec((None, 1, CONV_WIDTH), l3),
        pl.BlockSpec((None, 1, CONV_WIDTH), l3),
        pl.BlockSpec((None, 1, CONV_WIDTH), l3),
        _single((None, CONV_WIDTH, CONV_WIDTH), l3),
        pl.BlockSpec((None, 1, CONV_WIDTH), l3),
        pl.BlockSpec((None, SC_K, SC_WIDTH), l3),
    ]


def _mixer_weight_args(w_pool, pool_scale, conf_dw, conf_dw_b, conf_ln_g, conf_ln_b,
                       conf_pw, conf_pw_b, sc_conv):
    return (w_pool, _vec3(pool_scale), conf_dw, _vec3(conf_dw_b), _vec3(conf_ln_g),
            _vec3(conf_ln_b), conf_pw, _vec3(conf_pw_b), sc_conv)


def _layernorm_silu(cb, lng_ref, lnb_ref):
    mu = jnp.mean(cb, axis=-1, keepdims=True)
    xc = cb - mu
    var = jnp.mean(xc * xc, axis=-1, keepdims=True)
    return _silu(xc * lax.rsqrt(var + EPS) * lng_ref[...] + lnb_ref[...])


def _conv31_block(eg_ref, dw_ref, c0, t0, rows):
    base = _HALO_CONF - (CONF_K - 1)
    acc = None
    for r in range(_SUBLANE):
        p = None
        for q in range((CONF_K + base) // _SUBLANE + 1):
            k = _SUBLANE * q + r - base
            if 0 <= k < CONF_K:
                lo = t0 + _SUBLANE * q
                term = dw_ref[k:k + 1, c0:c0 + _LANE] * eg_ref[lo:lo + rows + _SUBLANE, c0:c0 + _LANE]
                p = term if p is None else p + term
        shifted = p[r:r + rows, :]
        acc = shifted if acc is None else acc + shifted
    return acc


def _mixers_prompt(z_ref, wpool_ref, pscale_ref, dw_ref, dwb_ref, lng_ref, lnb_ref,
                   pwbf_ref, pwb_ref, scw_ref, y_ref, nconf_ref, nsc_ref,
                   eu_ref, eg_ref, ev_ref, cb_ref, *, tm, s, tick):
    tick()
    ev_ref[_HALO_SHORT:_HALO_SHORT + tm, :] = (
        z_ref[:, _O_SC:_O_SC + SC_WIDTH] * z_ref[:, _O_SH:_O_SH + SC_WIDTH])
    cv = scw_ref[0:1, :] * ev_ref[_HALO_SHORT - 2:_HALO_SHORT - 2 + tm, :]
    cv = cv + scw_ref[1:2, :] * ev_ref[_HALO_SHORT - 1:_HALO_SHORT - 1 + tm, :]
    cv = cv + scw_ref[2:3, :] * ev_ref[_HALO_SHORT:_HALO_SHORT + tm, :]
    y_ref[:, _Y_C:_Y_C + SC_WIDTH] = (z_ref[:, _O_SB:_O_SB + SC_WIDTH] * cv).astype(_BF16)
    nsc_ref[...] = ev_ref[tm + _HALO_SHORT - (SC_K - 1):tm + _HALO_SHORT, :]
    ev_ref[0:_HALO_SHORT, :] = ev_ref[tm:tm + _HALO_SHORT, :]

    eu_ref[_HALO_POOL:_HALO_POOL + tm, :] = z_ref[:, _O_U:_O_U + POOL_WIDTH]
    pos = s * tm + lax.broadcasted_iota(jnp.int32, (tm, 1), 0)
    for g, w in enumerate(POOL_WINDOWS):
        if g % 2 == 0:
            tick()
        c0 = g * POOL_GW
        u = eu_ref[_HALO_POOL:_HALO_POOL + tm, c0:c0 + POOL_GW]
        wsum = u
        for k in range(1, w):
            wsum = wsum + eu_ref[_HALO_POOL - k:_HALO_POOL - k + tm, c0:c0 + POOL_GW]
        cnt = jnp.minimum(w, pos + 1).astype(_F32)
        d = wsum / cnt - u
        y_ref[:, _Y_A + c0:_Y_A + c0 + POOL_GW] = (
            _dot(d, wpool_ref[g]) * pscale_ref[:, c0:c0 + POOL_GW]).astype(_BF16)
    eu_ref[0:_HALO_POOL, :] = eu_ref[tm:tm + _HALO_POOL, :]

    eg_ref[_HALO_CONF:_HALO_CONF + tm, :] = (
        z_ref[:, _O_GA:_O_GA + CONV_WIDTH] * jax.nn.sigmoid(z_ref[:, _O_GB:_O_GB + CONV_WIDTH]))
    blocks = [(t0, c0) for t0 in range(0, tm, _CONV_ROWS) for c0 in range(0, CONV_WIDTH, _LANE)]
    for n, (t0, c0) in enumerate(blocks):
        if n % _CONV_BLOCKS_PER_TICK == 0:
            tick()
        cb_ref[t0:t0 + _CONV_ROWS, c0:c0 + _LANE] = (
            _conv31_block(eg_ref, dw_ref, c0, t0, _CONV_ROWS) + dwb_ref[:, c0:c0 + _LANE])
    tick()
    act = _layernorm_silu(cb_ref[...], lng_ref, lnb_ref).astype(_BF16)
    y_ref[:, _Y_B:_Y_B + CONV_WIDTH] = (_dot(act, pwbf_ref[...]) + pwb_ref[...]).astype(_BF16)
    nconf_ref[...] = eg_ref[tm + _HALO_CONF - (CONF_K - 1):tm + _HALO_CONF, :]
    eg_ref[0:_HALO_CONF, :] = eg_ref[tm:tm + _HALO_CONF, :]


def _mix_out_prompt_kernel(z_ref, h_ref, wpool_ref, pscale_ref, dw_ref, dwb_ref, lng_ref, lnb_ref,
                           pw_ref, pwb_ref, scw_ref, wout_ref,
                           o_ref, nconf_ref, nsc_ref,
                           eu_ref, eg_ref, ev_ref, cb_ref, y0_ref, y1_ref, wo_ref, pwbf_ref,
                           *, tm, n_s, n_tiles):
    i = pl.program_id(0)
    s = jnp.minimum(i, n_tiles - 1) % n_s

    @pl.when(i == 0)
    def _():
        wo_ref[...] = wout_ref[...].astype(_BF16)
        pwbf_ref[...] = pw_ref[...].astype(_BF16)
        eg_ref[_HALO_CONF + tm:_HALO_CONF + tm + _SUBLANE, :] = jnp.zeros((_SUBLANE, CONV_WIDTH), _F32)
        y1_ref[...] = jnp.zeros(y1_ref.shape, _BF16)

    @pl.when(s == 0)
    def _():
        eu_ref[0:_HALO_POOL, :] = jnp.zeros((_HALO_POOL, POOL_WIDTH), _F32)
        eg_ref[0:_HALO_CONF, :] = jnp.zeros((_HALO_CONF, CONV_WIDTH), _F32)
        ev_ref[0:_HALO_SHORT, :] = jnp.zeros((_HALO_SHORT, SC_WIDTH), _F32)

    def step(y_prev_ref, y_cur_ref):
        pending = list(range(0, D_MODEL, _OUT_COLS_PER_TICK))

        def tick():
            if pending:
                c0 = pending.pop(0)
                cols = slice(c0, c0 + _OUT_COLS_PER_TICK)
                o_ref[:, cols] = h_ref[:, cols] + _dot(y_prev_ref[...], wo_ref[:, cols])

        _mixers_prompt(z_ref, wpool_ref, pscale_ref, dw_ref, dwb_ref, lng_ref, lnb_ref,
                       pwbf_ref, pwb_ref, scw_ref, y_cur_ref, nconf_ref, nsc_ref,
                       eu_ref, eg_ref, ev_ref, cb_ref, tm=tm, s=s, tick=tick)
        while pending:
            tick()

    @pl.when(i % 2 == 0)
    def _():
        step(y1_ref, y0_ref)

    @pl.when(i % 2 == 1)
    def _():
        step(y0_ref, y1_ref)


def _mix_out_prompt(z, h, nb, layer, mix_w, w_out, *, tm):
    m = z.shape[0]
    n_tiles = m // tm
    n_s = n_tiles // nb
    cur = lambda i: (jnp.minimum(i, n_tiles - 1), 0)
    prev = lambda i: (jnp.maximum(i - 1, 0), 0)
    seq = lambda i: (jnp.minimum(i, n_tiles - 1) // n_s, 0, 0)
    return pl.pallas_call(
        functools.partial(_mix_out_prompt_kernel, tm=tm, n_s=n_s, n_tiles=n_tiles),
        out_shape=(
            jax.ShapeDtypeStruct((m, D_MODEL), _F32),
            jax.ShapeDtypeStruct((nb, CONF_K - 1, CONV_WIDTH), _F32),
            jax.ShapeDtypeStruct((nb, SC_K - 1, SC_WIDTH), _F32),
        ),
        grid=(n_tiles + 1,),
        in_specs=([pl.BlockSpec((tm, IN_COLS), cur), pl.BlockSpec((tm, D_MODEL), prev)]
                  + _mixer_weight_specs(layer)
                  + [_single((None, D_MODEL, D_MODEL), lambda i: (layer, 0, 0))]),
        out_specs=(
            pl.BlockSpec((tm, D_MODEL), prev),
            pl.BlockSpec((None, CONF_K - 1, CONV_WIDTH), seq),
            pl.BlockSpec((None, SC_K - 1, SC_WIDTH), seq),
        ),
        scratch_shapes=[
            pltpu.VMEM((_HALO_POOL + tm, POOL_WIDTH), _F32),
            pltpu.VMEM((_HALO_CONF + tm + _SUBLANE, CONV_WIDTH), _F32),
            pltpu.VMEM((_HALO_SHORT + tm, SC_WIDTH), _F32),
            pltpu.VMEM((tm, CONV_WIDTH), _F32),
            pltpu.VMEM((tm, D_MODEL), _BF16),
            pltpu.VMEM((tm, D_MODEL), _BF16),
            pltpu.VMEM((D_MODEL, D_MODEL), _BF16),
            pltpu.VMEM((CONV_WIDTH, CONV_WIDTH), _BF16),
        ],
        compiler_params=_params("arbitrary"),
        name="mix_out_prompt",
    )(z, h, *_mixer_weight_args(*mix_w), w_out)


def _mix_out_sample_kernel(z_ref, h_ref, spool_ref, sconf_ref, ssc_ref,
                           wpool_ref, pscale_ref, dw_ref, dwb_ref, lng_ref, lnb_ref,
                           pw_ref, pwb_ref, scw_ref, wout_ref,
                           o_ref, g_ref, v_ref):
    out = h_ref[...]
    for g, w in enumerate(POOL_WINDOWS):
        c0 = g * POOL_GW
        u = z_ref[:, _O_U + c0:_O_U + c0 + POOL_GW]
        wsum = u + jnp.sum(spool_ref[:, POOL_BUF - (w - 1):POOL_BUF, c0:c0 + POOL_GW], axis=1)
        d = wsum / jnp.float32(min(w, PAST_LEN + 1)) - u
        y_a = _dot(d, wpool_ref[g]) * pscale_ref[:, c0:c0 + POOL_GW]
        out = out + _dot(y_a, wout_ref[_Y_A + c0:_Y_A + c0 + POOL_GW, :])

    gl = z_ref[:, _O_GA:_O_GA + CONV_WIDTH] * jax.nn.sigmoid(z_ref[:, _O_GB:_O_GB + CONV_WIDTH])
    g_ref[...] = gl
    acc = dw_ref[CONF_K - 1:CONF_K, :] * gl
    acc = acc + jnp.sum(sconf_ref[...] * dw_ref[0:CONF_K - 1, :][None], axis=1)
    y_b = _dot(_layernorm_silu(acc + dwb_ref[...], lng_ref, lnb_ref), pw_ref[...]) + pwb_ref[...]
    out = out + _dot(y_b, wout_ref[_Y_B:_Y_B + CONV_WIDTH, :])

    v = z_ref[:, _O_SC:_O_SC + SC_WIDTH] * z_ref[:, _O_SH:_O_SH + SC_WIDTH]
    v_ref[...] = v
    cv = jnp.sum(ssc_ref[...] * scw_ref[0:SC_K - 1, :][None], axis=1) + scw_ref[SC_K - 1:SC_K, :] * v
    y_c = z_ref[:, _O_SB:_O_SB + SC_WIDTH] * cv
    o_ref[...] = out + _dot(y_c, wout_ref[_Y_C:_Y_C + SC_WIDTH, :])


def _mix_out_sample(z, h, state_pool, state_conf, state_sc, layer, mix_w, w_out):
    m = z.shape[0]
    full = lambda a: pl.BlockSpec(a.shape, lambda i: (0,) * a.ndim)
    state = lambda a: _single((None,) + a.shape[1:], lambda i: (layer, 0, 0, 0))
    return pl.pallas_call(
        _mix_out_sample_kernel,
        out_shape=(
            jax.ShapeDtypeStruct((m, D_MODEL), _F32),
            jax.ShapeDtypeStruct((m, CONV_WIDTH), _F32),
            jax.ShapeDtypeStruct((m, SC_WIDTH), _F32),
        ),
        grid=(1,),
        in_specs=([full(z), full(h), state(state_pool), state(state_conf), state(state_sc)]
                  + _mixer_weight_specs(layer)
                  + [_single((None, D_MODEL, D_MODEL), lambda i: (layer, 0, 0))]),
        out_specs=(
            pl.BlockSpec((m, D_MODEL), lambda i: (0, 0)),
            pl.BlockSpec((m, CONV_WIDTH), lambda i: (0, 0)),
            pl.BlockSpec((m, SC_WIDTH), lambda i: (0, 0)),
        ),
        compiler_params=_params("arbitrary"),
        name="mix_out_sample",
    )(z, h, state_pool, state_conf, state_sc, *_mixer_weight_args(*mix_w), w_out)


def _ffn_kernel(h_ref, hs_ref, g_ref, k_ref, st_hbm, wup_hbm, wdn_hbm,
                o_ref, os_ref, nf_ref, us_ref,
                hn_ref, ea_ref, eb_ref, ca_ref, cb_ref, wa_buf, wb_buf, wd_buf, sa_buf, sb_buf, sem,
                *, layer, tm, ts, sub, tf, n_c, n_s, n_tiles):
    i = pl.program_id(0)
    s = i % n_s

    def chunk_copies(tile, chunk, slot):
        col_a = pl.multiple_of(chunk * tf, tf)
        col_b = pl.multiple_of(D_FF + chunk * tf, tf)
        rows = pl.ds(pl.multiple_of(tile * ts, ts), ts)
        return (
            pltpu.make_async_copy(wup_hbm.at[:, pl.ds(col_a, tf)], wa_buf.at[slot], sem.at[0, slot]),
            pltpu.make_async_copy(wup_hbm.at[:, pl.ds(col_b, tf)], wb_buf.at[slot], sem.at[1, slot]),
            pltpu.make_async_copy(wdn_hbm.at[pl.ds(col_a, tf), :], wd_buf.at[slot], sem.at[2, slot]),
            pltpu.make_async_copy(st_hbm.at[layer, rows, :, pl.ds(col_a, tf)], sa_buf.at[slot], sem.at[3, slot]),
            pltpu.make_async_copy(st_hbm.at[layer, rows, :, pl.ds(col_b, tf)], sb_buf.at[slot], sem.at[4, slot]),
        )

    @pl.when(i == 0)
    def _():
        for cp in chunk_copies(0, 0, 0):
            cp.start()

    h = h_ref[...]
    hs = hs_ref[...]
    hn_ref[0:tm, :] = _rms(h, g_ref[...]).astype(_BF16)
    hn_ref[tm:tm + ts, :] = _rms(hs, g_ref[...]).astype(_BF16)
    o_ref[...] = h
    os_ref[...] = hs

    @pl.when(s == 0)
    def _():
        ca_ref[...] = jnp.zeros(ca_ref.shape, _F32)
        cb_ref[...] = jnp.zeros(cb_ref.shape, _F32)

    starts = list(range(0, tm, sub))
    blocks = [(r0, r0 + sub) for r0 in starts[:-1]] + [(starts[-1], tm + ts)]

    def conv(e_ref, k, r0, r1):
        lo, n = _HALO_SHORT + r0, r1 - r0
        out = k[0:1, :] * e_ref[lo - 2:lo - 2 + n, :]
        out = out + k[1:2, :] * e_ref[lo - 1:lo - 1 + n, :]
        return out + k[2:3, :] * e_ref[lo:lo + n, :]

    def conv_sample(e_ref, k, st):
        new = e_ref[_HALO_SHORT + tm:_HALO_SHORT + tm + ts, :]
        hist = jnp.sum(st * k[0:FFN_K - 1, :][None], axis=1)
        return hist + k[FFN_K - 1:FFN_K, :] * new

    def chunk(c, carry):
        step = i * n_c + c
        slot = step % 2
        for cp in chunk_copies(i, c, slot):
            cp.wait()

        @pl.when(step + 1 < n_tiles * n_c)
        def _():
            wrap = c + 1 == n_c
            for cp in chunk_copies(jnp.where(wrap, i + 1, i), jnp.where(wrap, 0, c + 1), 1 - slot):
                cp.start()

        col_a = pl.multiple_of(c * tf, tf)
        col_b = pl.multiple_of(D_FF + c * tf, tf)
        wa = wa_buf[slot]
        wb = wb_buf[slot]
        wd = wd_buf[slot]
        ka = k_ref[:, pl.ds(col_a, tf)]
        kb = k_ref[:, pl.ds(col_b, tf)]
        ea_ref[0:_HALO_SHORT, :] = ca_ref[c]
        eb_ref[0:_HALO_SHORT, :] = cb_ref[c]

        for r0, r1 in blocks:
            hn = hn_ref[r0:r1, :]
            ea_ref[_HALO_SHORT + r0:_HALO_SHORT + r1, :] = _dot(hn, wa)
            eb_ref[_HALO_SHORT + r0:_HALO_SHORT + r1, :] = _dot(hn, wb)

        for r0, r1 in blocks:
            r1p = min(r1, tm)
            act = _silu(conv(ea_ref, ka, r0, r1p)) * conv(eb_ref, kb, r0, r1p)
            if r1 > tm:
                act_s = (_silu(conv_sample(ea_ref, ka, sa_buf[slot]))
                         * conv_sample(eb_ref, kb, sb_buf[slot]))
                res = _dot(jnp.concatenate([act, act_s], axis=0).astype(_BF16), wd)
                o_ref[r0:tm, :] += res[0:tm - r0, :]
                os_ref[...] += res[tm - r0:tm - r0 + ts, :]
            else:
                o_ref[r0:r1, :] += _dot(act.astype(_BF16), wd)

        ca_ref[c] = ea_ref[tm:tm + _HALO_SHORT, :]
        cb_ref[c] = eb_ref[tm:tm + _HALO_SHORT, :]
        tail = slice(tm + _HALO_SHORT - (FFN_K - 1), tm + _HALO_SHORT)
        nf_ref[:, pl.ds(col_a, tf)] = ea_ref[tail, :]
        nf_ref[:, pl.ds(col_b, tf)] = eb_ref[tail, :]
        new = slice(_HALO_SHORT + tm, _HALO_SHORT + tm + ts)
        us_ref[:, pl.ds(col_a, tf)] = ea_ref[new, :]
        us_ref[:, pl.ds(col_b, tf)] = eb_ref[new, :]
        return carry

    lax.fori_loop(0, n_c, chunk, 0)


def _ffn(h, hs, nb, norm_g, w_up, ffn_conv, w_down, state_ffn, layer, *, tm, tf, sub):
    m = h.shape[0]
    n_tiles = m // tm
    n_s = n_tiles // nb
    n_c = D_FF // tf
    ts = hs.shape[0] // n_tiles
    assert ts * n_tiles == hs.shape[0] and ts % _BF16_ROWS == 0, (hs.shape, n_tiles)
    return pl.pallas_call(
        functools.partial(_ffn_kernel, layer=layer, tm=tm, ts=ts, sub=sub, tf=tf, n_c=n_c,
                          n_s=n_s, n_tiles=n_tiles),
        out_shape=(
            jax.ShapeDtypeStruct((m, D_MODEL), _F32),
            jax.ShapeDtypeStruct(hs.shape, _F32),
            jax.ShapeDtypeStruct((n_tiles, FFN_K - 1, 2 * D_FF), _F32),
            jax.ShapeDtypeStruct((hs.shape[0], 2 * D_FF), _F32),
        ),
        grid=(n_tiles,),
        in_specs=[
            _single((tm, D_MODEL), lambda i: (i, 0)),
            pl.BlockSpec((ts, D_MODEL), lambda i: (i, 0)),
            pl.BlockSpec((None, 1, D_MODEL), lambda i: (layer, 0, 0)),
            pl.BlockSpec((None, FFN_K, 2 * D_FF), lambda i: (layer, 0, 0)),
            pl.BlockSpec(memory_space=pl.ANY),
            pl.BlockSpec(memory_space=pl.ANY),
            pl.BlockSpec(memory_space=pl.ANY),
        ],
        out_specs=(
            pl.BlockSpec((tm, D_MODEL), lambda i: (i, 0)),
            pl.BlockSpec((ts, D_MODEL), lambda i: (i, 0)),
            pl.BlockSpec((None, FFN_K - 1, 2 * D_FF), lambda i: (i, 0, 0)),
            pl.BlockSpec((ts, 2 * D_FF), lambda i: (i, 0)),
        ),
        scratch_shapes=[
            pltpu.VMEM((tm + ts, D_MODEL), _BF16),
            pltpu.VMEM((_HALO_SHORT + tm + ts, tf), _F32),
            pltpu.VMEM((_HALO_SHORT + tm + ts, tf), _F32),
            pltpu.VMEM((n_c, _HALO_SHORT, tf), _F32),
            pltpu.VMEM((n_c, _HALO_SHORT, tf), _F32),
            pltpu.VMEM((2, D_MODEL, tf), _BF16),
            pltpu.VMEM((2, D_MODEL, tf), _BF16),
            pltpu.VMEM((2, tf, D_MODEL), _BF16),
            pltpu.VMEM((2, ts, FFN_K - 1, tf), _F32),
            pltpu.VMEM((2, ts, FFN_K - 1, tf), _F32),
            pltpu.SemaphoreType.DMA((5, 2)),
        ],
        compiler_params=_params("arbitrary"),
        name="ffn",
    )(h, hs, _vec3(norm_g), ffn_conv, state_ffn, w_up, w_down)


def _ple_kernel(h_ref, hs_ref, g_ref, gate_ref, p_ref, ps_ref, proj_ref, gf_ref, o_ref, os_ref,
                gate_bf, proj_bf, *, final, tm, sub, n_i):
    i = pl.program_id(0)

    @pl.when(i == 0)
    def _():
        gate_bf[...] = gate_ref[...].astype(_BF16)
        proj_bf[...] = proj_ref[...].astype(_BF16)

    def rows(h, p):
        hn = _rms(h, g_ref[...]).astype(_BF16)
        gate = jax.nn.sigmoid(_dot(hn, gate_bf[...]))
        out = h + _dot(p.astype(_BF16), proj_bf[...]) * gate
        return _rms(out, gf_ref[...]) if final else out

    @pl.when(i < n_i)
    def _():
        for r0 in range(0, tm, sub):
            o_ref[r0:r0 + sub, :] = rows(h_ref[r0:r0 + sub, :], p_ref[r0:r0 + sub, :])

    @pl.when(i == n_i)
    def _():
        os_ref[...] = rows(hs_ref[...], ps_ref[...])


def _ple(h, hs, norm_g, gate_w, p, ps, proj_w, layer, final_g, *, tm, sub, final):
    m = h.shape[0]
    ms = hs.shape[0]
    n_i = m // tm
    row = lambda i: (jnp.minimum(i, n_i - 1), 0)
    return pl.pallas_call(
        functools.partial(_ple_kernel, final=final, tm=tm, sub=sub, n_i=n_i),
        out_shape=(jax.ShapeDtypeStruct((m, D_MODEL), _F32), jax.ShapeDtypeStruct((ms, D_MODEL), _F32)),
        grid=(n_i + 1,),
        in_specs=[
            pl.BlockSpec((tm, D_MODEL), row),
            pl.BlockSpec((ms, D_MODEL), lambda i: (0, 0)),
            pl.BlockSpec((None, 1, D_MODEL), lambda i: (layer, 0, 0)),
            _single((None, D_MODEL, D_MODEL), lambda i: (layer, 0, 0)),
            pl.BlockSpec((None, tm, PLE_DIM), lambda i: (layer, jnp.minimum(i, n_i - 1), 0)),
            pl.BlockSpec((None, ms, PLE_DIM), lambda i: (layer, 0, 0)),
            _single((None, PLE_DIM, D_MODEL), lambda i: (layer, 0, 0)),
            pl.BlockSpec((1, D_MODEL), lambda i: (0, 0)),
        ],
        out_specs=(
            pl.BlockSpec((tm, D_MODEL), row),
            pl.BlockSpec((ms, D_MODEL), lambda i: (0, 0)),
        ),
        scratch_shapes=[
            pltpu.VMEM((D_MODEL, D_MODEL), _BF16),
            pltpu.VMEM((PLE_DIM, D_MODEL), _BF16),
        ],
        compiler_params=_params("arbitrary"),
        name="ple",
    )(h, hs, _vec3(norm_g), gate_w, p, ps, proj_w, final_g.reshape(1, -1))


def kernel(x_prompt, x_sample, p_prompt, p_sample, state_pool, state_conf, state_sc, state_ffn,
           norm_mix, w_in, w_pool, pool_scale, conf_dw, conf_dw_b, conf_ln_g, conf_ln_b,
           conf_pw, conf_pw_b, sc_conv, w_out, norm_ffn, w_up, ffn_conv, w_down,
           norm_ple, ple_gate, ple_proj, norm_final):
    nb, seq, _ = x_prompt.shape
    ns = x_sample.shape[0]
    depth = w_in.shape[0]
    hp = x_prompt.reshape(nb * seq, D_MODEL)
    hs = x_sample.reshape(ns, D_MODEL)
    pp = p_prompt.reshape(depth, nb * seq, PLE_DIM)
    ps = p_sample.reshape(depth, ns, PLE_DIM)
    mix_w = (w_pool, pool_scale, conf_dw, conf_dw_b, conf_ln_g, conf_ln_b, conf_pw, conf_pw_b, sc_conv)

    prompt_states = [[] for _ in range(4)]
    sample_rows = [[] for _ in range(4)]
    for i in range(depth):
        last = i == depth - 1

        z, zs, w_up_bf, w_down_bf = _norm_matmul(hp, hs, norm_mix, w_in, w_up, w_down, i,
                                                 tm=_TM_IN, tn=_TN_IN, sub=_SUB_IN)
        npool_p = z.reshape(nb, seq, IN_COLS)[:, seq - POOL_BUF:, :POOL_WIDTH]
        h1, nconf_p, nsc_p = _mix_out_prompt(z, hp, nb, i, mix_w, w_out, tm=_TM_MIX)
        h1s, g_s, v_s = _mix_out_sample(zs, hs, state_pool, state_conf, state_sc, i, mix_w, w_out)

        h2, h2s, nf, up_s = _ffn(h1, h1s, nb, norm_ffn, w_up_bf, ffn_conv, w_down_bf,
                                 state_ffn, i, tm=_TM_FFN, tf=_TF_FFN, sub=_SUB_FFN)
        tiles_per_seq = nf.shape[0] // nb
        nffn_p = nf[tiles_per_seq - 1::tiles_per_seq]

        hp, hs = _ple(h2, h2s, norm_ple, ple_gate, pp, ps, ple_proj, i, norm_final,
                      tm=_TM_PLE, sub=_SUB_PLE, final=last)

        for lst, val in zip(prompt_states, (npool_p, nconf_p, nsc_p, nffn_p)):
            lst.append(val)
        for lst, val in zip(sample_rows, (zs[:, :POOL_WIDTH], g_s, v_s, up_s)):
            lst.append(val)

    def shifted(state, rows):
        return jnp.concatenate([state[:, :, 1:], jnp.stack(rows)[:, :, None]], axis=2)

    new_p = [jnp.stack(l) for l in prompt_states]
    new_s = [shifted(st, rows) for st, rows in
             zip((state_pool, state_conf, state_sc, state_ffn), sample_rows)]
    y_prompt = hp.reshape(nb, seq, D_MODEL)
    y_sample = hs.reshape(ns, 1, D_MODEL)
    return (y_prompt, y_sample, new_p[0], new_s[0], new_p[1], new_s[1],
            new_p[2], new_s[2], new_p[3], new_s[3])
```

```python
import functools

import jax
import jax.numpy as jnp
from jax import lax
from jax.experimental import pallas as pl
from jax.experimental.pallas import tpu as pltpu

D_MODEL = 2048
POOL_WIDTH = 512
POOL_WINDOWS = (2, 4, 8, 16)
POOL_GW = 128
POOL_BUF = 15
CONV_WIDTH = 768
CONF_K = 31
SC_WIDTH = 768
SC_K = 3
FFN_K = 3
D_FF = 5632
PLE_DIM = 256
IN_COLS = POOL_WIDTH + 2 * CONV_WIDTH + 3 * SC_WIDTH
EPS = 1e-6
PAST_LEN = 16384

_O_U = 0
_O_GA = _O_U + POOL_WIDTH
_O_GB = _O_GA + CONV_WIDTH
_O_SB = _O_GB + CONV_WIDTH
_O_SC = _O_SB + SC_WIDTH
_O_SH = _O_SC + SC_WIDTH
_Y_A = 0
_Y_B = POOL_WIDTH
_Y_C = POOL_WIDTH + CONV_WIDTH

_VMEM_LIMIT = 56 * 1024 * 1024
_LANE = 128
_SUBLANE = 8
_BF16_ROWS = 16

_HALO_POOL = 16
_HALO_CONF = 32
_HALO_SHORT = 8

_TM_IN, _TN_IN, _SUB_IN = 512, IN_COLS // 2, 128
_STAGE_ROWS = 256
_TM_MIX = 256
_CONV_ROWS = 128
_CONV_BLOCKS_PER_TICK = 3
_OUT_COLS_PER_TICK = 256
_TM_FFN, _TF_FFN, _SUB_FFN = 1024, 512, 256
_TM_PLE, _SUB_PLE = 512, 256

_BF16 = jnp.bfloat16
_F32 = jnp.float32


def _params(*sem):
    return pltpu.CompilerParams(dimension_semantics=sem, vmem_limit_bytes=_VMEM_LIMIT)


def _rms(x, g):
    ms = jnp.mean(x * x, axis=-1, keepdims=True)
    return x * lax.rsqrt(ms + EPS) * g


def _dot(a, b):
    return jnp.dot(a, b, preferred_element_type=_F32)


def _sigmoid(x):
    return 0.5 * jnp.tanh(0.5 * x) + 0.5


def _silu(x):
    return x * _sigmoid(x)


def _single(block, index_map):
    return pl.BlockSpec(block, index_map, pipeline_mode=pl.Buffered(1))


def _vec3(a):
    return a.reshape(a.shape[0], 1, a.shape[1])


def _norm_matmul_kernel(x_ref, xs_ref, g_ref, w_hbm, wu_ref, wd_ref,
                        o_ref, os_ref, wub_ref, wdb_ref,
                        wb_ref, stage_ref, sem, *, layer, tm, tn, sub, n_i):
    j = pl.program_id(0)
    i = pl.program_id(1)

    @pl.when(i == 0)
    def _():
        rows = stage_ref.shape[1]
        col = pl.multiple_of(j * tn, _LANE)

        def piece(r, slot):
            return pltpu.make_async_copy(w_hbm.at[layer, pl.ds(r * rows, rows), pl.ds(col, tn)],
                                         stage_ref.at[slot], sem.at[slot])

        n = wb_ref.shape[0] // rows
        piece(0, 0).start()
        for r in range(n):
            piece(r, r % 2).wait()
            if r + 1 < n:
                piece(r + 1, (r + 1) % 2).start()
            wb_ref[r * rows:(r + 1) * rows, :] = stage_ref[r % 2].astype(_BF16)

    @pl.when(i < n_i)
    def _():
        for r0 in range(0, tm, sub):
            hn = _rms(x_ref[r0:r0 + sub, :], g_ref[...]).astype(_BF16)
            o_ref[r0:r0 + sub, :] = _dot(hn, wb_ref[...])
        wub_ref[...] = wu_ref[...].astype(_BF16)
        wdb_ref[...] = wd_ref[...].astype(_BF16)

    @pl.when(i == n_i)
    def _():
        os_ref[...] = _dot(_rms(xs_ref[...], g_ref[...]).astype(_BF16), wb_ref[...])


def _norm_matmul(x, xs, g, w, w_up, w_down, layer, *, tm, tn, sub):
    m, k = x.shape
    ms = xs.shape[0]
    n = w.shape[2]
    n_i = m // tm
    n_j = n // tn
    n_cast = n_j * n_i
    ru, rd = w_up.shape[1] // n_cast, w_down.shape[1] // n_cast
    assert ru * n_cast == w_up.shape[1] and rd * n_cast == w_down.shape[1]
    assert ru % _BF16_ROWS == 0 and rd % _BF16_ROWS == 0 and k % _STAGE_ROWS == 0
    row = lambda j, i: (jnp.minimum(i, n_i - 1), 0)
    cast_in = lambda j, i: (layer, j * n_i + jnp.minimum(i, n_i - 1), 0)
    cast_out = lambda j, i: (j * n_i + jnp.minimum(i, n_i - 1), 0)
    return pl.pallas_call(
        functools.partial(_norm_matmul_kernel, layer=layer, tm=tm, tn=tn, sub=sub, n_i=n_i),
        out_shape=(
            jax.ShapeDtypeStruct((m, n), _F32),
            jax.ShapeDtypeStruct((ms, n), _F32),
            jax.ShapeDtypeStruct(w_up.shape[1:], _BF16),
            jax.ShapeDtypeStruct(w_down.shape[1:], _BF16),
        ),
        grid=(n_j, n_i + 1),
        in_specs=[
            pl.BlockSpec((tm, k), row),
            pl.BlockSpec((ms, k), lambda j, i: (0, 0)),
            pl.BlockSpec((None, 1, k), lambda j, i: (layer, 0, 0)),
            pl.BlockSpec(memory_space=pl.ANY),
            pl.BlockSpec((None, ru, w_up.shape[2]), cast_in),
            pl.BlockSpec((None, rd, w_down.shape[2]), cast_in),
        ],
        out_specs=(
            pl.BlockSpec((tm, tn), lambda j, i: (jnp.minimum(i, n_i - 1), j)),
            pl.BlockSpec((ms, tn), lambda j, i: (0, j)),
            pl.BlockSpec((ru, w_up.shape[2]), cast_out),
            pl.BlockSpec((rd, w_down.shape[2]), cast_out),
        ),
        scratch_shapes=[
            pltpu.VMEM((k, tn), _BF16),
            pltpu.VMEM((2, _STAGE_ROWS, tn), _F32),
            pltpu.SemaphoreType.DMA((2,)),
        ],
        compiler_params=_params("arbitrary", "arbitrary"),
        name="in_proj",
    )(x, xs, _vec3(g), w, w_up, w_down)


def _mixer_weight_specs(layer):
    l3 = lambda *_: (layer, 0, 0)
    l4 = lambda *_: (layer, 0, 0, 0)
    return [
        _single((None, len(POOL_WINDOWS), POOL_GW, POOL_GW), l4),
        pl.BlockSpec((None, 1, POOL_WIDTH), l3),
        pl.BlockSpec((None, CONF_K, CONV_WIDTH), l3),
        pl.BlockSpec((None, 1, CONV_WIDTH), l3),
        pl.BlockSpec((None, 1, CONV_WIDTH), l3),
        pl.BlockSpec((None, 1, CONV_WIDTH), l3),
        _single((None, CONV_WIDTH, CONV_WIDTH), l3),
        pl.BlockSpec((None, 1, CONV_WIDTH), l3),
        pl.BlockSpec((None, SC_K, SC_WIDTH), l3),
    ]


def _mixer_weight_args(w_pool, pool_scale, conf_dw, conf_dw_b, conf_ln_g, conf_ln_b,
                       conf_pw, conf_pw_b, sc_conv):
    return (w_pool, _vec3(pool_scale), conf_dw, _vec3(conf_dw_b), _vec3(conf_ln_g),
            _vec3(conf_ln_b), conf_pw, _vec3(conf_pw_b), sc_conv)


def _layernorm_silu(cb, lng_ref, lnb_ref):
    mu = jnp.mean(cb, axis=-1, keepdims=True)
    xc = cb - mu
    var = jnp.mean(xc * xc, axis=-1, keepdims=True)
    return _silu(xc * lax.rsqrt(var + EPS) * lng_ref[...] + lnb_ref[...])


def _conv31_block(eg_ref, dw_ref, c0, t0, rows):
    base = _HALO_CONF - (CONF_K - 1)
    acc = None
    for r in range(_SUBLANE):
        p = None
        for q in range((CONF_K + base) // _SUBLANE + 1):
            k = _SUBLANE * q + r - base
            if 0 <= k < CONF_K:
                lo = t0 + _SUBLANE * q
                term = dw_ref[k:k + 1, c0:c0 + _LANE] * eg_ref[lo:lo + rows + _SUBLANE, c0:c0 + _LANE]
                p = term if p is None else p + term
        shifted = p[r:r + rows, :]
        acc = shifted if acc is None else acc + shifted
    return acc


def _mixers_prompt(z_ref, wpool_ref, pscale_ref, dw_ref, dwb_ref, lng_ref, lnb_ref,
                   pwbf_ref, pwb_ref, scw_ref, y_ref, nconf_ref, nsc_ref,
                   eu_ref, eg_ref, ev_ref, cb_ref, *, tm, s, tick):
    tick()
    ev_ref[_HALO_SHORT:_HALO_SHORT + tm, :] = (
        z_ref[:, _O_SC:_O_SC + SC_WIDTH] * z_ref[:, _O_SH:_O_SH + SC_WIDTH])
    cv = scw_ref[0:1, :] * ev_ref[_HALO_SHORT - 2:_HALO_SHORT - 2 + tm, :]
    cv = cv + scw_ref[1:2, :] * ev_ref[_HALO_SHORT - 1:_HALO_SHORT - 1 + tm, :]
    cv = cv + scw_ref[2:3, :] * ev_ref[_HALO_SHORT:_HALO_SHORT + tm, :]
    y_ref[:, _Y_C:_Y_C + SC_WIDTH] = (z_ref[:, _O_SB:_O_SB + SC_WIDTH] * cv).astype(_BF16)
    nsc_ref[...] = ev_ref[tm + _HALO_SHORT - (SC_K - 1):tm + _HALO_SHORT, :]
    ev_ref[0:_HALO_SHORT, :] = ev_ref[tm:tm + _HALO_SHORT, :]

    eu_ref[_HALO_POOL:_HALO_POOL + tm, :] = z_ref[:, _O_U:_O_U + POOL_WIDTH]
    pos = s * tm + lax.broadcasted_iota(jnp.int32, (tm, 1), 0)
    for g, w in enumerate(POOL_WINDOWS):
        if g % 2 == 0:
            tick()
        c0 = g * POOL_GW
        u = eu_ref[_HALO_POOL:_HALO_POOL + tm, c0:c0 + POOL_GW]
        wsum = u
        for k in range(1, w):
            wsum = wsum + eu_ref[_HALO_POOL - k:_HALO_POOL - k + tm, c0:c0 + POOL_GW]
        cnt = jnp.minimum(w, pos + 1).astype(_F32)
        d = wsum / cnt - u
        y_ref[:, _Y_A + c0:_Y_A + c0 + POOL_GW] = (
            _dot(d, wpool_ref[g]) * pscale_ref[:, c0:c0 + POOL_GW]).astype(_BF16)
    eu_ref[0:_HALO_POOL, :] = eu_ref[tm:tm + _HALO_POOL, :]

    eg_ref[_HALO_CONF:_HALO_CONF + tm, :] = (
        z_ref[:, _O_GA:_O_GA + CONV_WIDTH] * _sigmoid(z_ref[:, _O_GB:_O_GB + CONV_WIDTH]))
    blocks = [(t0, c0) for t0 in range(0, tm, _CONV_ROWS) for c0 in range(0, CONV_WIDTH, _LANE)]
    for n, (t0, c0) in enumerate(blocks):
        if n % _CONV_BLOCKS_PER_TICK == 0:
            tick()
        cb_ref[t0:t0 + _CONV_ROWS, c0:c0 + _LANE] = (
            _conv31_block(eg_ref, dw_ref, c0, t0, _CONV_ROWS) + dwb_ref[:, c0:c0 + _LANE])
    tick()
    act = _layernorm_silu(cb_ref[...], lng_ref, lnb_ref).astype(_BF16)
    y_ref[:, _Y_B:_Y_B + CONV_WIDTH] = (_dot(act, pwbf_ref[...]) + pwb_ref[...]).astype(_BF16)
    nconf_ref[...] = eg_ref[tm + _HALO_CONF - (CONF_K - 1):tm + _HALO_CONF, :]
    eg_ref[0:_HALO_CONF, :] = eg_ref[tm:tm + _HALO_CONF, :]


def _mix_out_prompt_kernel(z_ref, h_ref, wpool_ref, pscale_ref, dw_ref, dwb_ref, lng_ref, lnb_ref,
                           pw_ref, pwb_ref, scw_ref, wout_ref,
                           o_ref, nconf_ref, nsc_ref,
                           eu_ref, eg_ref, ev_ref, cb_ref, y0_ref, y1_ref, wo_ref, pwbf_ref,
                           *, tm, n_s, n_tiles):
    i = pl.program_id(0)
    s = jnp.minimum(i, n_tiles - 1) % n_s

    @pl.when(i == 0)
    def _():
        wo_ref[...] = wout_ref[...].astype(_BF16)
        pwbf_ref[...] = pw_ref[...].astype(_BF16)
        eg_ref[_HALO_CONF + tm:_HALO_CONF + tm + _SUBLANE, :] = jnp.zeros((_SUBLANE, CONV_WIDTH), _F32)
        y1_ref[...] = jnp.zeros(y1_ref.shape, _BF16)

    @pl.when(s == 0)
    def _():
        eu_ref[0:_HALO_POOL, :] = jnp.zeros((_HALO_POOL, POOL_WIDTH), _F32)
        eg_ref[0:_HALO_CONF, :] = jnp.zeros((_HALO_CONF, CONV_WIDTH), _F32)
        ev_ref[0:_HALO_SHORT, :] = jnp.zeros((_HALO_SHORT, SC_WIDTH), _F32)

    def step(y_prev_ref, y_cur_ref):
        pending = list(range(0, D_MODEL, _OUT_COLS_PER_TICK))

        def tick():
            if pending:
                c0 = pending.pop(0)
                cols = slice(c0, c0 + _OUT_COLS_PER_TICK)
                o_ref[:, cols] = h_ref[:, cols] + _dot(y_prev_ref[...], wo_ref[:, cols])

        _mixers_prompt(z_ref, wpool_ref, pscale_ref, dw_ref, dwb_ref, lng_ref, lnb_ref,
                       pwbf_ref, pwb_ref, scw_ref, y_cur_ref, nconf_ref, nsc_ref,
                       eu_ref, eg_ref, ev_ref, cb_ref, tm=tm, s=s, tick=tick)
        while pending:
            tick()

    @pl.when(i % 2 == 0)
    def _():
        step(y1_ref, y0_ref)

    @pl.when(i % 2 == 1)
    def _():
        step(y0_ref, y1_ref)


def _mix_out_prompt(z, h, nb, layer, mix_w, w_out, *, tm):
    m = z.shape[0]
    n_tiles = m // tm
    n_s = n_tiles // nb
    cur = lambda i: (jnp.minimum(i, n_tiles - 1), 0)
    prev = lambda i: (jnp.maximum(i - 1, 0), 0)
    seq = lambda i: (jnp.minimum(i, n_tiles - 1) // n_s, 0, 0)
    return pl.pallas_call(
        functools.partial(_mix_out_prompt_kernel, tm=tm, n_s=n_s, n_tiles=n_tiles),
        out_shape=(
            jax.ShapeDtypeStruct((m, D_MODEL), _F32),
            jax.ShapeDtypeStruct((nb, CONF_K - 1, CONV_WIDTH), _F32),
            jax.ShapeDtypeStruct((nb, SC_K - 1, SC_WIDTH), _F32),
        ),
        grid=(n_tiles + 1,),
        in_specs=([pl.BlockSpec((tm, IN_COLS), cur), pl.BlockSpec((tm, D_MODEL), prev)]
                  + _mixer_weight_specs(layer)
                  + [_single((None, D_MODEL, D_MODEL), lambda i: (layer, 0, 0))]),
        out_specs=(
            pl.BlockSpec((tm, D_MODEL), prev),
            pl.BlockSpec((None, CONF_K - 1, CONV_WIDTH), seq),
            pl.BlockSpec((None, SC_K - 1, SC_WIDTH), seq),
        ),
        scratch_shapes=[
            pltpu.VMEM((_HALO_POOL + tm, POOL_WIDTH), _F32),
            pltpu.VMEM((_HALO_CONF + tm + _SUBLANE, CONV_WIDTH), _F32),
            pltpu.VMEM((_HALO_SHORT + tm, SC_WIDTH), _F32),
            pltpu.VMEM((tm, CONV_WIDTH), _F32),
            pltpu.VMEM((tm, D_MODEL), _BF16),
            pltpu.VMEM((tm, D_MODEL), _BF16),
            pltpu.VMEM((D_MODEL, D_MODEL), _BF16),
            pltpu.VMEM((CONV_WIDTH, CONV_WIDTH), _BF16),
        ],
        compiler_params=_params("arbitrary"),
        name="mix_out_prompt",
    )(z, h, *_mixer_weight_args(*mix_w), w_out)


def _mix_out_sample_kernel(z_ref, h_ref, spool_ref, sconf_ref, ssc_ref,
                           wpool_ref, pscale_ref, dw_ref, dwb_ref, lng_ref, lnb_ref,
                           pw_ref, pwb_ref, scw_ref, wout_ref,
                           o_ref, g_ref, v_ref):
    out = h_ref[...]
    for g, w in enumerate(POOL_WINDOWS):
        c0 = g * POOL_GW
        u = z_ref[:, _O_U + c0:_O_U + c0 + POOL_GW]
        wsum = u + jnp.sum(spool_ref[:, POOL_BUF - (w - 1):POOL_BUF, c0:c0 + POOL_GW], axis=1)
        d = wsum / jnp.float32(min(w, PAST_LEN + 1)) - u
        y_a = _dot(d, wpool_ref[g]) * pscale_ref[:, c0:c0 + POOL_GW]
        out = out + _dot(y_a, wout_ref[_Y_A + c0:_Y_A + c0 + POOL_GW, :])

    gl = z_ref[:, _O_GA:_O_GA + CONV_WIDTH] * _sigmoid(z_ref[:, _O_GB:_O_GB + CONV_WIDTH])
    g_ref[...] = gl
    acc = dw_ref[CONF_K - 1:CONF_K, :] * gl
    acc = acc + jnp.sum(sconf_ref[...] * dw_ref[0:CONF_K - 1, :][None], axis=1)
    y_b = _dot(_layernorm_silu(acc + dwb_ref[...], lng_ref, lnb_ref), pw_ref[...]) + pwb_ref[...]
    out = out + _dot(y_b, wout_ref[_Y_B:_Y_B + CONV_WIDTH, :])

    v = z_ref[:, _O_SC:_O_SC + SC_WIDTH] * z_ref[:, _O_SH:_O_SH + SC_WIDTH]
    v_ref[...] = v
    cv = jnp.sum(ssc_ref[...] * scw_ref[0:SC_K - 1, :][None], axis=1) + scw_ref[SC_K - 1:SC_K, :] * v
    y_c = z_ref[:, _O_SB:_O_SB + SC_WIDTH] * cv
    o_ref[...] = out + _dot(y_c, wout_ref[_Y_C:_Y_C + SC_WIDTH, :])


def _mix_out_sample(z, h, state_pool, state_conf, state_sc, layer, mix_w, w_out):
    m = z.shape[0]
    full = lambda a: pl.BlockSpec(a.shape, lambda i: (0,) * a.ndim)
    state = lambda a: _single((None,) + a.shape[1:], lambda i: (layer, 0, 0, 0))
    return pl.pallas_call(
        _mix_out_sample_kernel,
        out_shape=(
            jax.ShapeDtypeStruct((m, D_MODEL), _F32),
            jax.ShapeDtypeStruct((m, CONV_WIDTH), _F32),
            jax.ShapeDtypeStruct((m, SC_WIDTH), _F32),
        ),
        grid=(1,),
        in_specs=([full(z), full(h), state(state_pool), state(state_conf), state(state_sc)]
                  + _mixer_weight_specs(layer)
                  + [_single((None, D_MODEL, D_MODEL), lambda i: (layer, 0, 0))]),
        out_specs=(
            pl.BlockSpec((m, D_MODEL), lambda i: (0, 0)),
            pl.BlockSpec((m, CONV_WIDTH), lambda i: (0, 0)),
            pl.BlockSpec((m, SC_WIDTH), lambda i: (0, 0)),
        ),
        compiler_params=_params("arbitrary"),
        name="mix_out_sample",
    )(z, h, state_pool, state_conf, state_sc, *_mixer_weight_args(*mix_w), w_out)


def _ffn_kernel(h_ref, hs_ref, g_ref, k_ref, st_hbm, wup_hbm, wdn_hbm,
                o_ref, os_ref, nf_ref, us_ref,
                hn_ref, ea_ref, eb_ref, ca_ref, cb_ref, wa_buf, wb_buf, wd_buf, sa_buf, sb_buf, sem,
                *, layer, tm, ts, sub, tf, n_c, n_s, n_tiles):
    i = pl.program_id(0)
    s = i % n_s

    def chunk_copies(tile, chunk, slot):
        col_a = pl.multiple_of(chunk * tf, tf)
        col_b = pl.multiple_of(D_FF + chunk * tf, tf)
        rows = pl.ds(pl.multiple_of(tile * ts, ts), ts)
        return (
            pltpu.make_async_copy(wup_hbm.at[:, pl.ds(col_a, tf)], wa_buf.at[slot], sem.at[0, slot]),
            pltpu.make_async_copy(wup_hbm.at[:, pl.ds(col_b, tf)], wb_buf.at[slot], sem.at[1, slot]),
            pltpu.make_async_copy(wdn_hbm.at[pl.ds(col_a, tf), :], wd_buf.at[slot], sem.at[2, slot]),
            pltpu.make_async_copy(st_hbm.at[layer, rows, :, pl.ds(col_a, tf)], sa_buf.at[slot], sem.at[3, slot]),
            pltpu.make_async_copy(st_hbm.at[layer, rows, :, pl.ds(col_b, tf)], sb_buf.at[slot], sem.at[4, slot]),
        )

    @pl.when(i == 0)
    def _():
        for cp in chunk_copies(0, 0, 0):
            cp.start()

    h = h_ref[...]
    hs = hs_ref[...]
    hn_ref[0:tm, :] = _rms(h, g_ref[...]).astype(_BF16)
    hn_ref[tm:tm + ts, :] = _rms(hs, g_ref[...]).astype(_BF16)
    o_ref[...] = h
    os_ref[...] = hs

    @pl.when(s == 0)
    def _():
        ca_ref[...] = jnp.zeros(ca_ref.shape, _F32)
        cb_ref[...] = jnp.zeros(cb_ref.shape, _F32)

    starts = list(range(0, tm, sub))
    blocks = [(r0, r0 + sub) for r0 in starts[:-1]] + [(starts[-1], tm + ts)]

    def conv(e_ref, k, r0, r1):
        lo, n = _HALO_SHORT + r0, r1 - r0
        out = k[0:1, :] * e_ref[lo - 2:lo - 2 + n, :]
        out = out + k[1:2, :] * e_ref[lo - 1:lo - 1 + n, :]
        return out + k[2:3, :] * e_ref[lo:lo + n, :]

    def conv_sample(e_ref, k, st):
        new = e_ref[_HALO_SHORT + tm:_HALO_SHORT + tm + ts, :]
        hist = jnp.sum(st * k[0:FFN_K - 1, :][None], axis=1)
        return hist + k[FFN_K - 1:FFN_K, :] * new

    def chunk(c, carry):
        step = i * n_c + c
        slot = step % 2
        for cp in chunk_copies(i, c, slot):
            cp.wait()

        @pl.when(step + 1 < n_tiles * n_c)
        def _():
            wrap = c + 1 == n_c
            for cp in chunk_copies(jnp.where(wrap, i + 1, i), jnp.where(wrap, 0, c + 1), 1 - slot):
                cp.start()

        col_a = pl.multiple_of(c * tf, tf)
        col_b = pl.multiple_of(D_FF + c * tf, tf)
        wa = wa_buf[slot]
        wb = wb_buf[slot]
        wd = wd_buf[slot]
        ka = k_ref[:, pl.ds(col_a, tf)]
        kb = k_ref[:, pl.ds(col_b, tf)]
        ea_ref[0:_HALO_SHORT, :] = ca_ref[c]
        eb_ref[0:_HALO_SHORT, :] = cb_ref[c]

        for r0, r1 in blocks:
            hn = hn_ref[r0:r1, :]
            ea_ref[_HALO_SHORT + r0:_HALO_SHORT + r1, :] = _dot(hn, wa)
            eb_ref[_HALO_SHORT + r0:_HALO_SHORT + r1, :] = _dot(hn, wb)

        for r0, r1 in blocks:
            r1p = min(r1, tm)
            act = _silu(conv(ea_ref, ka, r0, r1p)) * conv(eb_ref, kb, r0, r1p)
            if r1 > tm:
                act_s = (_silu(conv_sample(ea_ref, ka, sa_buf[slot]))
                         * conv_sample(eb_ref, kb, sb_buf[slot]))
                res = _dot(jnp.concatenate([act, act_s], axis=0).astype(_BF16), wd)
                o_ref[r0:tm, :] += res[0:tm - r0, :]
                os_ref[...] += res[tm - r0:tm - r0 + ts, :]
            else:
                o_ref[r0:r1, :] += _dot(act.astype(_BF16), wd)

        ca_ref[c] = ea_ref[tm:tm + _HALO_SHORT, :]
        cb_ref[c] = eb_ref[tm:tm + _HALO_SHORT, :]
        tail = slice(tm + _HALO_SHORT - (FFN_K - 1), tm + _HALO_SHORT)
        nf_ref[:, pl.ds(col_a, tf)] = ea_ref[tail, :]
        nf_ref[:, pl.ds(col_b, tf)] = eb_ref[tail, :]
        new = slice(_HALO_SHORT + tm, _HALO_SHORT + tm + ts)
        us_ref[:, pl.ds(col_a, tf)] = ea_ref[new, :]
        us_ref[:, pl.ds(col_b, tf)] = eb_ref[new, :]
        return carry

    lax.fori_loop(0, n_c, chunk, 0)


def _ffn(h, hs, nb, norm_g, w_up, ffn_conv, w_down, state_ffn, layer, *, tm, tf, sub):
    m = h.shape[0]
    n_tiles = m // tm
    n_s = n_tiles // nb
    n_c = D_FF // tf
    ts = hs.shape[0] // n_tiles
    assert ts * n_tiles == hs.shape[0] and ts % _BF16_ROWS == 0, (hs.shape, n_tiles)
    return pl.pallas_call(
        functools.partial(_ffn_kernel, layer=layer, tm=tm, ts=ts, sub=sub, tf=tf, n_c=n_c,
                          n_s=n_s, n_tiles=n_tiles),
        out_shape=(
            jax.ShapeDtypeStruct((m, D_MODEL), _F32),
            jax.ShapeDtypeStruct(hs.shape, _F32),
            jax.ShapeDtypeStruct((n_tiles, FFN_K - 1, 2 * D_FF), _F32),
            jax.ShapeDtypeStruct((hs.shape[0], 2 * D_FF), _F32),
        ),
        grid=(n_tiles,),
        in_specs=[
            _single((tm, D_MODEL), lambda i: (i, 0)),
            pl.BlockSpec((ts, D_MODEL), lambda i: (i, 0)),
            pl.BlockSpec((None, 1, D_MODEL), lambda i: (layer, 0, 0)),
            pl.BlockSpec((None, FFN_K, 2 * D_FF), lambda i: (layer, 0, 0)),
            pl.BlockSpec(memory_space=pl.ANY),
            pl.BlockSpec(memory_space=pl.ANY),
            pl.BlockSpec(memory_space=pl.ANY),
        ],
        out_specs=(
            pl.BlockSpec((tm, D_MODEL), lambda i: (i, 0)),
            pl.BlockSpec((ts, D_MODEL), lambda i: (i, 0)),
            pl.BlockSpec((None, FFN_K - 1, 2 * D_FF), lambda i: (i, 0, 0)),
            pl.BlockSpec((ts, 2 * D_FF), lambda i: (i, 0)),
        ),
        scratch_shapes=[
            pltpu.VMEM((tm + ts, D_MODEL), _BF16),
            pltpu.VMEM((_HALO_SHORT + tm + ts, tf), _F32),
            pltpu.VMEM((_HALO_SHORT + tm + ts, tf), _F32),
            pltpu.VMEM((n_c, _HALO_SHORT, tf), _F32),
            pltpu.VMEM((n_c, _HALO_SHORT, tf), _F32),
            pltpu.VMEM((2, D_MODEL, tf), _BF16),
            pltpu.VMEM((2, D_MODEL, tf), _BF16),
            pltpu.VMEM((2, tf, D_MODEL), _BF16),
            pltpu.VMEM((2, ts, FFN_K - 1, tf), _F32),
            pltpu.VMEM((2, ts, FFN_K - 1, tf), _F32),
            pltpu.SemaphoreType.DMA((5, 2)),
        ],
        compiler_params=_params("arbitrary"),
        name="ffn",
    )(h, hs, _vec3(norm_g), ffn_conv, state_ffn, w_up, w_down)


def _ple_kernel(h_ref, hs_ref, g_ref, gate_ref, p_ref, ps_ref, proj_ref, gf_ref, o_ref, os_ref,
                gate_bf, proj_bf, *, final, tm, sub, n_i):
    i = pl.program_id(0)

    @pl.when(i == 0)
    def _():
        gate_bf[...] = gate_ref[...].astype(_BF16)
        proj_bf[...] = proj_ref[...].astype(_BF16)

    def rows(h, p):
        hn = _rms(h, g_ref[...]).astype(_BF16)
        gate = _sigmoid(_dot(hn, gate_bf[...]))
        out = h + _dot(p.astype(_BF16), proj_bf[...]) * gate
        return _rms(out, gf_ref[...]) if final else out

    @pl.when(i < n_i)
    def _():
        for r0 in range(0, tm, sub):
            o_ref[r0:r0 + sub, :] = rows(h_ref[r0:r0 + sub, :], p_ref[r0:r0 + sub, :])

    @pl.when(i == n_i)
    def _():
        os_ref[...] = rows(hs_ref[...], ps_ref[...])


def _ple(h, hs, norm_g, gate_w, p, ps, proj_w, layer, final_g, *, tm, sub, final):
    m = h.shape[0]
    ms = hs.shape[0]
    n_i = m // tm
    row = lambda i: (jnp.minimum(i, n_i - 1), 0)
    return pl.pallas_call(
        functools.partial(_ple_kernel, final=final, tm=tm, sub=sub, n_i=n_i),
        out_shape=(jax.ShapeDtypeStruct((m, D_MODEL), _F32), jax.ShapeDtypeStruct((ms, D_MODEL), _F32)),
        grid=(n_i + 1,),
        in_specs=[
            pl.BlockSpec((tm, D_MODEL), row),
            pl.BlockSpec((ms, D_MODEL), lambda i: (0, 0)),
            pl.BlockSpec((None, 1, D_MODEL), lambda i: (layer, 0, 0)),
            _single((None, D_MODEL, D_MODEL), lambda i: (layer, 0, 0)),
            pl.BlockSpec((None, tm, PLE_DIM), lambda i: (layer, jnp.minimum(i, n_i - 1), 0)),
            pl.BlockSpec((None, ms, PLE_DIM), lambda i: (layer, 0, 0)),
            _single((None, PLE_DIM, D_MODEL), lambda i: (layer, 0, 0)),
            pl.BlockSpec((1, D_MODEL), lambda i: (0, 0)),
        ],
        out_specs=(
            pl.BlockSpec((tm, D_MODEL), row),
            pl.BlockSpec((ms, D_MODEL), lambda i: (0, 0)),
        ),
        scratch_shapes=[
            pltpu.VMEM((D_MODEL, D_MODEL), _BF16),
            pltpu.VMEM((PLE_DIM, D_MODEL), _BF16),
        ],
        compiler_params=_params("arbitrary"),
        name="ple",
    )(h, hs, _vec3(norm_g), gate_w, p, ps, proj_w, final_g.reshape(1, -1))


def kernel(x_prompt, x_sample, p_prompt, p_sample, state_pool, state_conf, state_sc, state_ffn,
           norm_mix, w_in, w_pool, pool_scale, conf_dw, conf_dw_b, conf_ln_g, conf_ln_b,
           conf_pw, conf_pw_b, sc_conv, w_out, norm_ffn, w_up, ffn_conv, w_down,
           norm_ple, ple_gate, ple_proj, norm_final):
    nb, seq, _ = x_prompt.shape
    ns = x_sample.shape[0]
    depth = w_in.shape[0]
    hp = x_prompt.reshape(nb * seq, D_MODEL)
    hs = x_sample.reshape(ns, D_MODEL)
    pp = p_prompt.reshape(depth, nb * seq, PLE_DIM)
    ps = p_sample.reshape(depth, ns, PLE_DIM)
    mix_w = (w_pool, pool_scale, conf_dw, conf_dw_b, conf_ln_g, conf_ln_b, conf_pw, conf_pw_b, sc_conv)

    prompt_states = [[] for _ in range(4)]
    sample_rows = [[] for _ in range(4)]
    for i in range(depth):
        last = i == depth - 1

        z, zs, w_up_bf, w_down_bf = _norm_matmul(hp, hs, norm_mix, w_in, w_up, w_down, i,
                                                 tm=_TM_IN, tn=_TN_IN, sub=_SUB_IN)
        npool_p = z.reshape(nb, seq, IN_COLS)[:, seq - POOL_BUF:, :POOL_WIDTH]
        h1, nconf_p, nsc_p = _mix_out_prompt(z, hp, nb, i, mix_w, w_out, tm=_TM_MIX)
        h1s, g_s, v_s = _mix_out_sample(zs, hs, state_pool, state_conf, state_sc, i, mix_w, w_out)

        h2, h2s, nf, up_s = _ffn(h1, h1s, nb, norm_ffn, w_up_bf, ffn_conv, w_down_bf,
                                 state_ffn, i, tm=_TM_FFN, tf=_TF_FFN, sub=_SUB_FFN)
        tiles_per_seq = nf.shape[0] // nb
        nffn_p = nf[tiles_per_seq - 1::tiles_per_seq]

        hp, hs = _ple(h2, h2s, norm_ple, ple_gate, pp, ps, ple_proj, i, norm_final,
                      tm=_TM_PLE, sub=_SUB_PLE, final=last)

        for lst, val in zip(prompt_states, (npool_p, nconf_p, nsc_p, nffn_p)):
            lst.append(val)
        for lst, val in zip(sample_rows, (zs[:, :POOL_WIDTH], g_s, v_s, up_s)):
            lst.append(val)

    def shifted(state, rows):
        return jnp.concatenate([state[:, :, 1:], jnp.stack(rows)[:, :, None]], axis=2)

    new_p = [jnp.stack(l) for l in prompt_states]
    new_s = [shifted(st, rows) for st, rows in
             zip((state_pool, state_conf, state_sc, state_ffn), sample_rows)]
    y_prompt = hp.reshape(nb, seq, D_MODEL)
    y_sample = hs.reshape(ns, 1, D_MODEL)
    return (y_prompt, y_sample, new_p[0], new_s[0], new_p[1], new_s[1],
            new_p[2], new_s[2], new_p[3], new_s[3])
```
